```python
import math
import jax, jax.numpy as jnp
from jax import lax
import numpy as np

D_MODEL = 2048
BATCH = 4
SEQ = 4096
DEPTH = 4

D_MIX = D_MODEL
D_FF = 2 * D_MODEL
NORM_EPS = 1e-6
GATED_NORM_EPS = 1e-5
Q_BLOCK = 128
ROPE_THETA = 500000.0
ROT_FRACTION = 4
FOX_HEADS = 4
FOX_HEAD_DIM = 128
FGATE_BIAS_INIT = 3.0
SSM_D_INNER = D_MIX // 2
SSM_HEAD_DIM = 64
SSM_HEADS = SSM_D_INNER // SSM_HEAD_DIM
SSM_GROUPS = 2
SSM_STATE = 128
SSM_CONV = 4
SSM_CHUNK = 128
SSM_CONV_CH = SSM_D_INNER + 2 * SSM_GROUPS * SSM_STATE
DSA_HEADS = 4
DSA_HEAD_DIM = 128
DSA_Q_LORA = 512
IDX_HEADS = 8
IDX_HEAD_DIM = 64
TOPK_MAX = 256
SPLIT_SIZES = (
    FOX_HEADS * FOX_HEAD_DIM,
    FOX_HEADS * FOX_HEAD_DIM,
    FOX_HEADS * FOX_HEAD_DIM,
    FOX_HEADS,
    SSM_D_INNER,
    SSM_CONV_CH,
    SSM_HEADS,
    DSA_Q_LORA,
    DSA_HEAD_DIM,
    DSA_HEAD_DIM,
    IDX_HEAD_DIM,
    IDX_HEADS,
)
N_IN = (3 * FOX_HEADS * FOX_HEAD_DIM + FOX_HEADS + SSM_D_INNER + SSM_CONV_CH + SSM_HEADS
        + DSA_Q_LORA + 2 * DSA_HEAD_DIM + IDX_HEAD_DIM + IDX_HEADS)

kernel_name = 'hybrid_fox_ssd_dsa_macaron_trunk'


def rms_norm(x, g, eps=NORM_EPS):
    xf = x.astype(jnp.float32)
    y = xf * lax.rsqrt(jnp.mean(xf * xf, axis=-1, keepdims=True) + eps)
    return (y * g.astype(jnp.float32)).astype(x.dtype)


def swiglu(h, w_gate, w_up, w_down):
    return (jax.nn.silu(h @ w_gate) * (h @ w_up)) @ w_down


def rope_cos_sin(seq_len, rot_dim):
    inv = ROPE_THETA ** (-jnp.arange(0, rot_dim, 2, dtype=jnp.float32) / rot_dim)
    ang = jnp.arange(seq_len, dtype=jnp.float32)[:, None] * inv[None, :]
    return jnp.cos(ang), jnp.sin(ang)


def apply_partial_rope(x, cos, sin):
    half = cos.shape[-1]
    rot = 2 * half
    xf = x.astype(jnp.float32)
    x1, x2 = xf[..., :half], xf[..., half:rot]
    c, s = cos[None, :, None, :], sin[None, :, None, :]
    out = jnp.concatenate([x1 * c - x2 * s, x2 * c + x1 * s, xf[..., rot:]], axis=-1)
    return out.astype(x.dtype)


def to_blocks(a, nb):
    return jnp.moveaxis(a.reshape(a.shape[0], nb, Q_BLOCK, *a.shape[2:]), 1, 0)


def from_blocks(a):
    a = jnp.moveaxis(a, 0, 1)
    return a.reshape(a.shape[0], a.shape[1] * a.shape[2], *a.shape[3:])


def fox_attention(q, k, v, log_f):
    B, S, H, D = q.shape
    nb = S // Q_BLOCK
    scale = D ** -0.5
    c = jnp.cumsum(log_f, axis=1).transpose(0, 2, 1)
    kpos = jnp.arange(S)

    def block(args):
        qb, cb, tb = args
        logits = (jnp.einsum('bqhd,bkhd->bhqk', qb, k).astype(jnp.float32) * scale
                  + cb[:, :, :, None] - c[:, :, None, :])
        causal = kpos[None, :] <= tb[:, None]
        logits = jnp.where(causal[None, None], logits, -jnp.inf)
        p = jax.nn.softmax(logits, axis=-1)
        return jnp.einsum('bhqk,bkhd->bqhd', p.astype(v.dtype), v)

    cb = c.reshape(B, H, nb, Q_BLOCK).transpose(2, 0, 1, 3)
    out = lax.map(block, (to_blocks(q, nb), cb, kpos.reshape(nb, Q_BLOCK)))
    return from_blocks(out)


def causal_depthwise_conv(x, w, b):
    C = x.shape[-1]
    y = lax.conv_general_dilated(x, w[:, None, :], window_strides=(1,),
                                 padding=[(w.shape[0] - 1, 0)],
                                 dimension_numbers=('NWC', 'WIO', 'NWC'),
                                 feature_group_count=C)
    return y + b


def ssd_chunked(x, a, b, c):
    Bsz, S, H, P = x.shape
    Lc = SSM_CHUNK
    nc = S // Lc
    rep = H // b.shape[2]
    N = b.shape[-1]
    b = jnp.repeat(b, rep, axis=2).reshape(Bsz, nc, Lc, H, N)
    c = jnp.repeat(c, rep, axis=2).reshape(Bsz, nc, Lc, H, N)
    x = x.reshape(Bsz, nc, Lc, H, P)
    a = a.reshape(Bsz, nc, Lc, H).transpose(0, 3, 1, 2)
    a_cs = jnp.cumsum(a, axis=-1)
    tril = jnp.tril(jnp.ones((Lc, Lc), dtype=bool))
    seg = jnp.exp(jnp.where(tril, a_cs[..., :, None] - a_cs[..., None, :], -jnp.inf))
    scores = jnp.einsum('bclhn,bcshn->bhcls', c, b) * seg
    y_diag = jnp.einsum('bhcls,bcshp->bclhp', scores, x)
    decay_states = jnp.exp(a_cs[..., -1:] - a_cs)
    states = jnp.einsum('bclhn,bhcl,bclhp->bchpn', b, decay_states, x)
    chunk_decay = jnp.exp(a_cs[..., -1])

    def step(h, inp):
        s_c, d_c = inp
        return h * d_c[:, :, None, None] + s_c, h

    h0 = jnp.zeros((Bsz, H, P, N), jnp.float32)
    _, h_in = lax.scan(step, h0, (jnp.moveaxis(states, 1, 0), jnp.moveaxis(chunk_decay, 2, 0)))
    h_in = jnp.moveaxis(h_in, 0, 1)
    y_off = jnp.einsum('bclhn,bchpn,bhcl->bclhp', c, h_in, jnp.exp(a_cs))
    return (y_diag + y_off).reshape(Bsz, S, H, P)


def gated_group_rmsnorm(y, z, g):
    B, S, Dn = y.shape
    u = (y.astype(jnp.float32) * jax.nn.silu(z.astype(jnp.float32))).reshape(B, S, SSM_GROUPS, Dn // SSM_GROUPS)
    u = u * lax.rsqrt(jnp.mean(u * u, axis=-1, keepdims=True) + GATED_NORM_EPS)
    return (u.reshape(B, S, Dn) * g.astype(jnp.float32)).astype(z.dtype)


def mamba2_mixer(z, xbc, dt_raw, conv_w, conv_b, dt_bias, a_log, d_skip, norm_g):
    B, S, _ = xbc.shape
    xbc = jax.nn.silu(causal_depthwise_conv(xbc, conv_w, conv_b))
    xs, bm, cm = jnp.split(xbc, [SSM_D_INNER, SSM_D_INNER + SSM_GROUPS * SSM_STATE], axis=-1)
    xs = xs.reshape(B, S, SSM_HEADS, SSM_HEAD_DIM).astype(jnp.float32)
    bm = bm.reshape(B, S, SSM_GROUPS, SSM_STATE).astype(jnp.float32)
    cm = cm.reshape(B, S, SSM_GROUPS, SSM_STATE).astype(jnp.float32)
    dt = jax.nn.softplus(dt_raw.astype(jnp.float32) + dt_bias.astype(jnp.float32))
    a = -jnp.exp(a_log.astype(jnp.float32))
    y = ssd_chunked(xs * dt[..., None], dt * a, bm, cm)
    y = y + xs * d_skip.astype(jnp.float32)[:, None]
    return gated_group_rmsnorm(y.reshape(B, S, SSM_D_INNER), z, norm_g)


def dsa_attention(q, k, v, q_idx, k_idx, w_idx, topk):
    B, S, H, D = q.shape
    nb = S // Q_BLOCK
    scale = D ** -0.5
    w_scale = (IDX_HEADS ** -0.5) * (IDX_HEAD_DIM ** -0.5)
    kpos = jnp.arange(S)
    k_idx_f = k_idx.astype(jnp.float32)

    def block(args):
        qb, qib, wb, tb = args
        sc = jnp.einsum('bqhe,bke->bqhk', qib.astype(jnp.float32), k_idx_f)
        idx_score = jnp.einsum('bqh,bqhk->bqk', wb.astype(jnp.float32) * w_scale, jax.nn.relu(sc))
        causal = kpos[None, :] <= tb[:, None]
        idx_score = jnp.where(causal[None], idx_score, -jnp.inf)
        _, sel = lax.top_k(idx_score, topk)
        k_sel = jax.vmap(lambda kb, ib: kb[ib])(k, sel)
        v_sel = jax.vmap(lambda vb, ib: vb[ib])(v, sel)
        logits = jnp.einsum('bqhd,bqkd->bhqk', qb, k_sel).astype(jnp.float32) * scale
        valid = sel <= tb[None, :, None]
        logits = jnp.where(valid[:, None], logits, -jnp.inf)
        p = jax.nn.softmax(logits, axis=-1)
        return jnp.einsum('bhqk,bqkd->bqhd', p.astype(v.dtype), v_sel)

    out = lax.map(block, (to_blocks(q, nb), to_blocks(q_idx, nb), to_blocks(w_idx, nb),
                          kpos.reshape(nb, Q_BLOCK)))
    return from_blocks(out)


def setup_inputs(seed: int = 0) -> dict:
    key = jax.random.key(seed)
    ks = iter(jax.random.split(key, 40))
    L = DEPTH

    def nrm(shape, scale):
        return jax.random.normal(next(ks), shape, jnp.float32) * scale

    def gain(shape):
        return 1.0 + nrm(shape, 0.02)

    dt0 = jnp.exp(jax.random.uniform(next(ks), (L, SSM_HEADS), jnp.float32,
                                     minval=math.log(1e-3), maxval=math.log(1e-1)))
    dt_bias = dt0 + jnp.log(-jnp.expm1(-dt0))
    a_log = jnp.log(jax.random.uniform(next(ks), (L, SSM_HEADS), jnp.float32, minval=1.0, maxval=16.0))
    return {
        'x': nrm((BATCH, SEQ, D_MODEL), 1.0),
        'ffn1_norm': gain((L, D_MODEL)),
        'ffn1_w_gate': nrm((L, D_MODEL, D_FF), D_MODEL ** -0.5),
        'ffn1_w_up': nrm((L, D_MODEL, D_FF), D_MODEL ** -0.5),
        'ffn1_w_down': nrm((L, D_FF, D_MODEL), D_FF ** -0.5),
        'mix_norm': gain((L, D_MODEL)),
        'w_in': nrm((L, D_MODEL, N_IN), D_MODEL ** -0.5),
        'fox_fgate_b': FGATE_BIAS_INIT + nrm((L, FOX_HEADS), 0.5),
        'fox_q_norm': gain((L, FOX_HEAD_DIM)),
        'fox_k_norm': gain((L, FOX_HEAD_DIM)),
        'ssm_conv_w': nrm((L, SSM_CONV, SSM_CONV_CH), SSM_CONV ** -0.5),
        'ssm_conv_b': nrm((L, SSM_CONV_CH), 0.02),
        'ssm_dt_bias': dt_bias,
        'ssm_a_log': a_log,
        'ssm_d': gain((L, SSM_HEADS)),
        'ssm_norm': gain((L, SSM_D_INNER)),
        'dsa_cq_norm': gain((L, DSA_Q_LORA)),
        'dsa_w_uq': nrm((L, DSA_Q_LORA, DSA_HEADS * DSA_HEAD_DIM), DSA_Q_LORA ** -0.5),
        'dsa_w_uq_idx': nrm((L, DSA_Q_LORA, IDX_HEADS * IDX_HEAD_DIM), DSA_Q_LORA ** -0.5),
        'dsa_q_norm': gain((L, DSA_HEAD_DIM)),
        'dsa_k_norm': gain((L, DSA_HEAD_DIM)),
        'dsa_kidx_norm': gain((L, IDX_HEAD_DIM)),
        'w_out': nrm((L, D_MIX, D_MODEL), 0.5 * D_MIX ** -0.5),
        'ffn2_norm': gain((L, D_MODEL)),
        'ffn2_w_gate': nrm((L, D_MODEL, D_FF), D_MODEL ** -0.5),
        'ffn2_w_up': nrm((L, D_MODEL, D_FF), D_MODEL ** -0.5),
        'ffn2_w_down': nrm((L, D_FF, D_MODEL), D_FF ** -0.5),
    }


def reference(x, ffn1_norm, ffn1_w_gate, ffn1_w_up, ffn1_w_down, mix_norm, w_in,
              fox_fgate_b, fox_q_norm, fox_k_norm, ssm_conv_w, ssm_conv_b, ssm_dt_bias,
              ssm_a_log, ssm_d, ssm_norm, dsa_cq_norm, dsa_w_uq, dsa_w_uq_idx, dsa_q_norm,
              dsa_k_norm, dsa_kidx_norm, w_out, ffn2_norm, ffn2_w_gate, ffn2_w_up, ffn2_w_down):
    Bsz, S, _ = x.shape
    topk = min(TOPK_MAX, S // 4)
    offs = np.cumsum(np.array(SPLIT_SIZES))[:-1].tolist()
    cos_a, sin_a = rope_cos_sin(S, DSA_HEAD_DIM // ROT_FRACTION)
    cos_i, sin_i = rope_cos_sin(S, IDX_HEAD_DIM // ROT_FRACTION)

    for l in range(DEPTH):
        x = x + 0.5 * swiglu(rms_norm(x, ffn1_norm[l]), ffn1_w_gate[l], ffn1_w_up[l], ffn1_w_down[l])

        h = rms_norm(x, mix_norm[l])
        proj = h @ w_in[l]
        (f_q, f_k, f_v, f_g, m_z, m_xbc, m_dt,
         d_cq, d_k, d_v, d_ki, d_wi) = jnp.split(proj, offs, axis=-1)

        fq = rms_norm(f_q.reshape(Bsz, S, FOX_HEADS, FOX_HEAD_DIM), fox_q_norm[l])
        fk = rms_norm(f_k.reshape(Bsz, S, FOX_HEADS, FOX_HEAD_DIM), fox_k_norm[l])
        fv = f_v.reshape(Bsz, S, FOX_HEADS, FOX_HEAD_DIM)
        log_f = jax.nn.log_sigmoid(f_g.astype(jnp.float32) + fox_fgate_b[l].astype(jnp.float32))
        fox_o = fox_attention(fq, fk, fv, log_f)

        ssm_o = mamba2_mixer(m_z, m_xbc, m_dt, ssm_conv_w[l], ssm_conv_b[l], ssm_dt_bias[l],
                             ssm_a_log[l], ssm_d[l], ssm_norm[l])

        cq = rms_norm(d_cq, dsa_cq_norm[l])
        dq = rms_norm((cq @ dsa_w_uq[l]).reshape(Bsz, S, DSA_HEADS, DSA_HEAD_DIM), dsa_q_norm[l])
        dq = apply_partial_rope(dq, cos_a, sin_a)
        dk = apply_partial_rope(rms_norm(d_k[:, :, None, :], dsa_k_norm[l]), cos_a, sin_a)[:, :, 0]
        qi = apply_partial_rope((cq @ dsa_w_uq_idx[l]).reshape(Bsz, S, IDX_HEADS, IDX_HEAD_DIM), cos_i, sin_i)
        ki = apply_partial_rope(rms_norm(d_ki[:, :, None, :], dsa_kidx_norm[l]), cos_i, sin_i)[:, :, 0]
        dsa_o = dsa_attention(dq, dk, d_v, qi, ki, d_wi, topk)

        mixed = jnp.concatenate([fox_o.reshape(Bsz, S, -1), ssm_o, dsa_o.reshape(Bsz, S, -1)], axis=-1)
        x = x + mixed @ w_out[l]

        x = x + 0.5 * swiglu(rms_norm(x, ffn2_norm[l]), ffn2_w_gate[l], ffn2_w_up[l], ffn2_w_down[l])
    return x
```

```python
import functools
import math

import jax
import jax.numpy as jnp
from jax import lax
from jax.experimental import pallas as pl
from jax.experimental.pallas import tpu as pltpu

F32 = jnp.float32
BF16 = jnp.bfloat16
I32 = jnp.int32

D_MODEL = 2048
D_FF = 2 * D_MODEL
NORM_EPS = 1e-6
GATED_NORM_EPS = 1e-5
ROPE_THETA = 500000.0
ROT_FRACTION = 4
HEAD_DIM = 128
FOX_HEADS = 4
SSM_D_INNER = D_MODEL // 2
SSM_HEAD_DIM = 64
SSM_HEADS = SSM_D_INNER // SSM_HEAD_DIM
SSM_GROUPS = 2
SSM_STATE = 128
SSM_CONV = 4
SSM_CHUNK = 128
SSM_CONV_CH = SSM_D_INNER + 2 * SSM_GROUPS * SSM_STATE
DSA_HEADS = 4
DSA_Q_LORA = 512
IDX_HEADS = 8
IDX_HEAD_DIM = 64
TOPK_MAX = 256
FOX_W = FOX_HEADS * HEAD_DIM
DSA_W = DSA_HEADS * HEAD_DIM
IDX_W = IDX_HEADS * 128

LANES = 128
SUBLANES = 8
V7X_VMEM_BUDGET = 56 * 1024 * 1024

P_XBC = 0
P_FQKV = P_XBC + SSM_CONV_CH
P_Z = P_FQKV + 3 * FOX_W
P_TAIL = P_Z + SSM_D_INNER
T_CQ = 0
T_DK = T_CQ + DSA_Q_LORA
T_DV = T_DK + HEAD_DIM
T_SM = T_DV + HEAD_DIM
TAIL_W = 1024
NP = P_TAIL + TAIL_W
SM_KI = 0
SM_FG = IDX_HEAD_DIM
SM_DT = SM_FG + FOX_HEADS
SM_WI = SM_DT + SSM_HEADS

NEG_BIG = -1e30
INT_MIN = -2 ** 31


def _cparams(sem, vmem=V7X_VMEM_BUDGET):
    return pltpu.CompilerParams(dimension_semantics=sem, vmem_limit_bytes=vmem)


def _sigmoid(x):
    return 1.0 / (1.0 + jnp.exp(-x))


def _softplus(x):
    return jnp.maximum(x, 0.0) + jnp.log(1.0 + jnp.exp(-jnp.abs(x)))


def _rms(x, eps):
    return x * lax.rsqrt(jnp.mean(x * x, axis=-1, keepdims=True) + eps)


def _dot_nt(a, b):
    return lax.dot_general(a, b, (((1,), (1,)), ((), ())), preferred_element_type=F32)


def _dot(a, b):
    return jnp.dot(a, b, preferred_element_type=F32)


def _dot_f32(a, b):
    return jnp.dot(a, b, preferred_element_type=F32, precision=lax.Precision.HIGHEST)


def _ffn_kernel(x_ref, g_ref, wg_ref, wu_ref, wd_ref, o_ref, h_ref, *, nj):
    j = pl.program_id(1)

    @pl.when(j == 0)
    def _():
        h_ref[...] = (_rms(x_ref[...], NORM_EPS) * g_ref[...]).astype(BF16)
        o_ref[...] = jnp.zeros_like(o_ref)

    h = h_ref[...]
    gate = _dot(h, wg_ref[...])
    up = _dot(h, wu_ref[...])
    a = (gate * _sigmoid(gate) * up).astype(BF16)
    o_ref[...] += _dot(a, wd_ref[...])

    @pl.when(j == nj - 1)
    def _():
        o_ref[...] = x_ref[...] + 0.5 * o_ref[...]


def _ffn(x, g, wg, wu, wd, *, tm=512, tf=512):
    T, D = x.shape
    F = wg.shape[1]
    nj = F // tf
    return pl.pallas_call(
        functools.partial(_ffn_kernel, nj=nj),
        grid=(T // tm, nj),
        in_specs=[
            pl.BlockSpec((tm, D), lambda i, j: (i, 0)),
            pl.BlockSpec((1, D), lambda i, j: (0, 0)),
            pl.BlockSpec((D, tf), lambda i, j: (0, j)),
            pl.BlockSpec((D, tf), lambda i, j: (0, j)),
            pl.BlockSpec((tf, D), lambda i, j: (j, 0)),
        ],
        out_specs=pl.BlockSpec((tm, D), lambda i, j: (i, 0)),
        out_shape=jax.ShapeDtypeStruct((T, D), F32),
        scratch_shapes=[pltpu.VMEM((tm, D), BF16)],
        compiler_params=_cparams(("parallel", "arbitrary")),
        name="ffn",
    )(x, g, wg, wu, wd)


def _in_proj_kernel(x_ref, g_ref, w_ref, o_ref, h_ref):
    @pl.when(pl.program_id(1) == 0)
    def _():
        h_ref[...] = (_rms(x_ref[...], NORM_EPS) * g_ref[...]).astype(BF16)

    o_ref[...] = _dot(h_ref[...], w_ref[...])


def _in_proj(x, g, w, *, tm=512, tn=1024):
    T, D = x.shape
    N = w.shape[1]
    return pl.pallas_call(
        _in_proj_kernel,
        grid=(T // tm, N // tn),
        in_specs=[
            pl.BlockSpec((tm, D), lambda i, j: (i, 0)),
            pl.BlockSpec((1, D), lambda i, j: (0, 0)),
            pl.BlockSpec((D, tn), lambda i, j: (0, j)),
        ],
        out_specs=pl.BlockSpec((tm, tn), lambda i, j: (i, j)),
        out_shape=jax.ShapeDtypeStruct((T, N), F32),
        scratch_shapes=[pltpu.VMEM((tm, D), BF16)],
        compiler_params=_cparams(("parallel", "arbitrary")),
        name="in_proj",
    )(x, g, w)


def _rope(x, c, s_lo, s_hi, half):
    n = x.shape[-1]
    return x * c + pltpu.roll(x, n - half, 1) * s_lo + pltpu.roll(x, half, 1) * s_hi


def _prep_kernel(fqkv_ref, tail_ref, fqg_ref, fkg_ref, fgb_ref, cqg_ref, wuq_ref, wuqi_ref,
                 dqg_ref, dkg_ref, kig_ref, ca_ref, sa1_ref, sa2_ref, ci_ref, si1_ref, si2_ref,
                 fq_ref, fk_ref, fv_ref, sm_ref, dq_ref, qi_ref, dk_ref, dv_ref, ki_ref, *, w_scale):
    for h in range(FOX_HEADS):
        lo, hi = h * HEAD_DIM, (h + 1) * HEAD_DIM
        q = fqkv_ref[:, lo:hi]
        k = fqkv_ref[:, FOX_W + lo:FOX_W + hi]
        fq_ref[:, lo:hi] = (_rms(q, NORM_EPS) * fqg_ref[...]).astype(BF16)
        fk_ref[:, lo:hi] = (_rms(k, NORM_EPS) * fkg_ref[...]).astype(BF16)
    fv_ref[...] = fqkv_ref[:, 2 * FOX_W:3 * FOX_W].astype(BF16)

    sm = tail_ref[:, T_SM:T_SM + LANES]
    lane = lax.broadcasted_iota(I32, sm.shape, 1)
    zf = sm + fgb_ref[...]
    logf = jnp.minimum(zf, 0.0) - jnp.log(1.0 + jnp.exp(-jnp.abs(zf)))
    is_fg = (lane >= SM_FG) & (lane < SM_FG + FOX_HEADS)
    sm_ref[...] = jnp.where(is_fg, logf, sm * w_scale)

    is_ki = lane < IDX_HEAD_DIM
    ms = jnp.sum(jnp.where(is_ki, sm * sm, 0.0), axis=-1, keepdims=True) * (1.0 / IDX_HEAD_DIM)
    kin = sm * lax.rsqrt(ms + NORM_EPS) * kig_ref[...]
    kin = _rope(kin, ci_ref[...], si1_ref[...], si2_ref[...], IDX_HEAD_DIM // ROT_FRACTION // 2)
    ki_ref[...] = kin.astype(BF16)

    dk = _rms(tail_ref[:, T_DK:T_DK + HEAD_DIM], NORM_EPS) * dkg_ref[...]
    dk_ref[...] = _rope(dk, ca_ref[...], sa1_ref[...], sa2_ref[...], HEAD_DIM // ROT_FRACTION // 2).astype(BF16)
    dv_ref[...] = tail_ref[:, T_DV:T_DV + HEAD_DIM].astype(BF16)

    cq = (_rms(tail_ref[:, T_CQ:T_CQ + DSA_Q_LORA], NORM_EPS) * cqg_ref[...]).astype(BF16)
    uq = _dot(cq, wuq_ref[...])
    uqi = _dot(cq, wuqi_ref[...])
    for h in range(DSA_HEADS):
        lo, hi = h * HEAD_DIM, (h + 1) * HEAD_DIM
        qh = _rms(uq[:, lo:hi], NORM_EPS) * dqg_ref[...]
        dq_ref[:, lo:hi] = _rope(qh, ca_ref[...], sa1_ref[...], sa2_ref[...],
                                 HEAD_DIM // ROT_FRACTION // 2).astype(BF16)
    for p in range(IDX_W // LANES):
        lo, hi = p * LANES, (p + 1) * LANES
        qi_ref[:, lo:hi] = _rope(uqi[:, lo:hi], ci_ref[...], si1_ref[...], si2_ref[...],
                                 IDX_HEAD_DIM // ROT_FRACTION // 2).astype(BF16)


def _prep(proj, params, tables, S, *, tm=256):
    T = proj.shape[0]
    ns = S // tm
    w_scale = (IDX_HEADS ** -0.5) * (IDX_HEAD_DIM ** -0.5)
    row = lambda w: pl.BlockSpec((1, w), lambda i: (0, 0))
    tab = pl.BlockSpec((tm, LANES), lambda i: (i % ns, 0))
    out_bf = lambda w: jax.ShapeDtypeStruct((T, w), BF16)
    out_blk = lambda w: pl.BlockSpec((tm, w), lambda i: (i, 0))
    return pl.pallas_call(
        functools.partial(_prep_kernel, w_scale=w_scale),
        grid=(T // tm,),
        in_specs=[
            pl.BlockSpec((tm, 3 * FOX_W), lambda i: (i, P_FQKV // (3 * FOX_W))),
            pl.BlockSpec((tm, TAIL_W), lambda i: (i, P_TAIL // TAIL_W)),
            row(LANES), row(LANES), row(LANES), row(DSA_Q_LORA),
            pl.BlockSpec((DSA_Q_LORA, DSA_W), lambda i: (0, 0)),
            pl.BlockSpec((DSA_Q_LORA, IDX_W), lambda i: (0, 0)),
            row(LANES), row(LANES), row(LANES),
            tab, tab, tab, tab, tab, tab,
        ],
        out_specs=[out_blk(FOX_W), out_blk(FOX_W), out_blk(FOX_W), out_blk(LANES),
                   out_blk(DSA_W), out_blk(IDX_W), out_blk(HEAD_DIM), out_blk(HEAD_DIM),
                   out_blk(LANES)],
        out_shape=[out_bf(FOX_W), out_bf(FOX_W), out_bf(FOX_W),
                   jax.ShapeDtypeStruct((T, LANES), F32),
                   out_bf(DSA_W), out_bf(IDX_W), out_bf(HEAD_DIM), out_bf(HEAD_DIM),
                   out_bf(LANES)],
        compiler_params=_cparams(("parallel",)),
        name="prep",
    )(proj, proj, *params, *tables)


def _cumsum_kernel(x_ref, o_ref, *, ch):
    S = x_ref.shape[0]
    r = lax.broadcasted_iota(I32, (ch, ch), 0)
    c = lax.broadcasted_iota(I32, (ch, ch), 1)
    tri = jnp.where(c <= r, 1.0, 0.0).astype(F32)

    def body(i, carry):
        off = pl.multiple_of(i * ch, ch)
        cs = _dot_f32(tri, x_ref[pl.ds(off, ch), :]) + carry
        o_ref[pl.ds(off, ch), :] = cs
        return cs[ch - 1:ch, :]

    lax.fori_loop(0, S // ch, body, jnp.zeros((1, x_ref.shape[1]), F32))


def _cumsum(x3, *, ch=128):
    B, S, W = x3.shape
    return pl.pallas_call(
        functools.partial(_cumsum_kernel, ch=ch),
        grid=(B,),
        in_specs=[pl.BlockSpec((None, S, W), lambda b: (b, 0, 0))],
        out_specs=pl.BlockSpec((None, S, W), lambda b: (b, 0, 0)),
        out_shape=jax.ShapeDtypeStruct((B, S, W), F32),
        compiler_params=_cparams(("parallel",)),
        name="cumsum",
    )(x3)


def _fox_kernel(q_ref, k_ref, v_ref, cc_ref, cr_ref, o_ref, *, tq, tk, scale):
    i = pl.program_id(1)
    nkv = (i + 1) * (tq // tk)
    row = i * tq + lax.broadcasted_iota(I32, (tq, tk), 0)
    col = lax.broadcasted_iota(I32, (tq, tk), 1)
    for h in range(FOX_HEADS):
        lo, hi = h * HEAD_DIM, (h + 1) * HEAD_DIM
        q = q_ref[:, lo:hi]
        cc = cc_ref[:, SM_FG + h:SM_FG + h + 1]

        def body(j, carry, h=h, lo=lo, hi=hi, q=q, cc=cc):
            m, l, acc = carry
            off = pl.multiple_of(j * tk, tk)
            k = k_ref[pl.ds(off, tk), lo:hi]
            v = v_ref[pl.ds(off, tk), lo:hi]
            s = _dot_nt(q, k) * scale + (cc - cr_ref[h:h + 1, pl.ds(off, tk)])
            s = jnp.where(col + off <= row, s, -jnp.inf)
            m_new = jnp.maximum(m, jnp.max(s, axis=-1, keepdims=True))
            p = jnp.exp(s - m_new)
            alpha = jnp.exp(m - m_new)
            l = alpha * l + jnp.sum(p, axis=-1, keepdims=True)
            acc = alpha * acc + _dot(p.astype(BF16), v)
            return m_new, l, acc

        m0 = jnp.full((tq, 1), -jnp.inf, F32)
        l0 = jnp.zeros((tq, 1), F32)
        a0 = jnp.zeros((tq, HEAD_DIM), F32)
        m, l, acc = lax.fori_loop(0, nkv, body, (m0, l0, a0))
        o_ref[:, lo:hi] = (acc / l).astype(o_ref.dtype)


def _fox(q, k, v, cum, c_row, *, tq=256, tk=256):
    B, S, W = q.shape
    return pl.pallas_call(
        functools.partial(_fox_kernel, tq=tq, tk=tk, scale=HEAD_DIM ** -0.5),
        grid=(B, S // tq),
        in_specs=[
            pl.BlockSpec((None, tq, W), lambda b, i: (b, i, 0)),
            pl.BlockSpec((None, S, W), lambda b, i: (b, 0, 0)),
            pl.BlockSpec((None, S, W), lambda b, i: (b, 0, 0)),
            pl.BlockSpec((None, tq, LANES), lambda b, i: (b, i, 0)),
            pl.BlockSpec((None, FOX_HEADS, S), lambda b, i: (b, 0, 0)),
        ],
        out_specs=pl.BlockSpec((None, tq, W), lambda b, i: (b, i, 0)),
        out_shape=jax.ShapeDtypeStruct((B, S, W), BF16),
        compiler_params=_cparams(("parallel", "arbitrary")),
        name="fox",
    )(q, k, v, cum, c_row)


def _ssd_kernel(xbc_ref, z_ref, sm_ref, cw_ref, cb_ref, dtbl_ref, alogl_ref, dtbr_ref, alogr_ref,
                dsk_ref, ng_ref, o_ref, xpad, state, ybuf):
    c = pl.program_id(1)
    L = SSM_CHUNK
    PAD = SUBLANES

    @pl.when(c == 0)
    def _():
        xpad[0:PAD, :] = jnp.zeros((PAD, SSM_CONV_CH), F32)
        state[...] = jnp.zeros_like(state)

    @pl.when(c > 0)
    def _():
        xpad[0:PAD, :] = xpad[L:L + PAD, :]

    xpad[PAD:PAD + L, :] = xbc_ref[...]
    conv = cb_ref[...] + cw_ref[0:1, :] * xpad[PAD - 3:PAD - 3 + L, :]
    for kk in range(1, SSM_CONV):
        conv = conv + cw_ref[kk:kk + 1, :] * xpad[PAD - 3 + kk:PAD - 3 + kk + L, :]
    xc = conv * _sigmoid(conv)
    gn = SSM_GROUPS * SSM_STATE

    sm = sm_ref[...]
    dt_c = _softplus(sm + dtbl_ref[...])
    a_c = dt_c * (-jnp.exp(alogl_ref[...]))
    dt_r = _softplus(sm.T + dtbr_ref[...])
    a_r = dt_r * (-jnp.exp(alogr_ref[...]))
    ri = lax.broadcasted_iota(I32, (L, L), 0)
    ci = lax.broadcasted_iota(I32, (L, L), 1)
    tril = ci <= ri
    acs_c = _dot_f32(jnp.where(tril, 1.0, 0.0).astype(F32), a_c)
    acs_r = _dot_f32(a_r, jnp.where(ri <= ci, 1.0, 0.0).astype(F32))

    for g in range(SSM_GROUPS):
        bg = xc[:, SSM_D_INNER + g * SSM_STATE:SSM_D_INNER + (g + 1) * SSM_STATE]
        cg = xc[:, SSM_D_INNER + gn + g * SSM_STATE:SSM_D_INNER + gn + (g + 1) * SSM_STATE]
        cg_b = cg.astype(BF16)
        gmat = _dot_nt(cg_b, bg.astype(BF16))
        bgt = bg.T
        for hh in range(SSM_HEADS // SSM_GROUPS):
            h = g * (SSM_HEADS // SSM_GROUPS) + hh
            lo, hi = h * SSM_HEAD_DIM, (h + 1) * SSM_HEAD_DIM
            colv = acs_c[:, SM_DT + h:SM_DT + h + 1]
            rowv = acs_r[SM_DT + h:SM_DT + h + 1, :]
            last = rowv[:, L - 1:L]
            seg = jnp.exp(jnp.where(tril, colv - rowv, -jnp.inf))
            x_h = xc[:, lo:hi]
            xdt = (x_h * dt_c[:, SM_DT + h:SM_DT + h + 1]).astype(BF16)
            st = state[h]
            y = _dot((gmat * seg).astype(BF16), xdt) + _dot(cg_b, st.astype(BF16)) * jnp.exp(colv)
            dec = jnp.exp(last - rowv)
            state[h] = st * jnp.exp(last) + _dot((bgt * dec).astype(BF16), xdt)
            ybuf[:, lo:hi] = y + x_h * dsk_ref[:, lo:hi]

    zz = z_ref[...]
    u = ybuf[...] * (zz * _sigmoid(zz))
    gw = SSM_D_INNER // SSM_GROUPS
    for g in range(SSM_GROUPS):
        ug = _rms(u[:, g * gw:(g + 1) * gw], GATED_NORM_EPS)
        o_ref[:, g * gw:(g + 1) * gw] = (ug * ng_ref[:, g * gw:(g + 1) * gw]).astype(o_ref.dtype)


def _ssd(proj3, params):
    B, S, _ = proj3.shape
    L = SSM_CHUNK
    full = lambda a: pl.BlockSpec(a.shape, lambda b, c: (0,) * a.ndim)
    return pl.pallas_call(
        _ssd_kernel,
        grid=(B, S // L),
        in_specs=[
            pl.BlockSpec((None, L, SSM_CONV_CH), lambda b, c: (b, c, P_XBC // SSM_CONV_CH)),
            pl.BlockSpec((None, L, SSM_D_INNER), lambda b, c: (b, c, P_Z // SSM_D_INNER)),
            pl.BlockSpec((None, L, LANES), lambda b, c: (b, c, (P_TAIL + T_SM) // LANES)),
        ] + [full(a) for a in params],
        out_specs=pl.BlockSpec((None, L, SSM_D_INNER), lambda b, c: (b, c, 0)),
        out_shape=jax.ShapeDtypeStruct((B, S, SSM_D_INNER), BF16),
        scratch_shapes=[
            pltpu.VMEM((L + SUBLANES, SSM_CONV_CH), F32),
            pltpu.VMEM((SSM_HEADS, SSM_STATE, SSM_HEAD_DIM), F32),
            pltpu.VMEM((L, SSM_D_INNER), F32),
        ],
        compiler_params=_cparams(("parallel", "arbitrary")),
        name="ssd",
    )(proj3, proj3, proj3, *params)


def _dsa_kernel(q_ref, qi_ref, wi_ref, k_ref, v_ref, ki_ref, o_ref, key_ref, *, tq, tk, topk, scale):
    i = pl.program_id(1)
    nk = (i + 1) * (tq // tk)
    row = i * tq + lax.broadcasted_iota(I32, (tq, tk), 0)
    col = lax.broadcasted_iota(I32, (tq, tk), 1)
    imin = jnp.int32(INT_MIN)
    ngrp = tk // LANES

    def p1(j, _):
        off = pl.multiple_of(j * tk, tk)
        kit = ki_ref[pl.ds(off, tk), :]
        acc = jnp.zeros((tq, tk), F32)
        for h in range(IDX_HEADS):
            s = _dot_nt(qi_ref[:, h * LANES:(h + 1) * LANES], kit)
            acc = acc + wi_ref[:, SM_WI + h:SM_WI + h + 1] * jnp.maximum(s, 0.0)
        acc = jnp.where(acc == 0.0, 0.0, acc)
        bits = pltpu.bitcast(acc, I32)
        key = bits ^ ((bits >> 31) & jnp.int32(0x7FFFFFFF))
        key_ref[:, pl.ds(off, tk)] = jnp.where(col + off <= row, key, imin)
        return 0

    lax.fori_loop(0, nk, p1, 0)

    def count(pred):
        def body(j, cnt):
            off = pl.multiple_of(j * tk, tk)
            m = jnp.where(pred(key_ref[:, pl.ds(off, tk)]), 1.0, 0.0)
            part = m[:, 0:LANES]
            for g in range(1, ngrp):
                part = part + m[:, g * LANES:(g + 1) * LANES]
            return cnt + part
        cnt = lax.fori_loop(0, nk, body, jnp.zeros((tq, LANES), F32))
        return jnp.sum(cnt, axis=-1, keepdims=True)

    kf = float(topk)

    def search(b, carry):
        prefix, c_ge = carry
        cand_u = prefix | lax.shift_left(jnp.int32(1), 31 - b)
        cand_s = cand_u ^ imin
        cnt = count(lambda kt: kt >= cand_s)
        ok = cnt >= kf
        return jnp.where(ok, cand_u, prefix), jnp.where(ok, cnt, c_ge)

    prefix, c_ge = lax.fori_loop(0, 32, search, (jnp.zeros((tq, 1), I32), jnp.zeros((tq, 1), F32)))
    thr = prefix ^ imin

    tie_row = (prefix != 0) & (c_ge > kf)
    n_tie = jnp.sum(jnp.where(tie_row, 1.0, 0.0))

    @pl.when(n_tie > 0.0)
    def _():
        need = kf - count(lambda kt: kt > thr)
        rr = lax.broadcasted_iota(I32, (tk, tk), 0)
        cc = lax.broadcasted_iota(I32, (tk, tk), 1)
        upper = jnp.where(rr < cc, 1.0, 0.0).astype(BF16)

        def fix(j, run):
            off = pl.multiple_of(j * tk, tk)
            kt = key_ref[:, pl.ds(off, tk)]
            eq = (kt == thr) & tie_row
            eqf = jnp.where(eq, 1.0, 0.0)
            rank = run + _dot(eqf.astype(BF16), upper)
            drop = eq & (rank >= need)
            key_ref[:, pl.ds(off, tk)] = jnp.where(drop, imin, kt)
            return run + jnp.sum(eqf, axis=-1, keepdims=True)

        lax.fori_loop(0, nk, fix, jnp.zeros((tq, 1), F32))

    thr_eff = jnp.maximum(thr, imin + 1)

    def att(j, carry):
        off = pl.multiple_of(j * tk, tk)
        sel = key_ref[:, pl.ds(off, tk)] >= thr_eff
        k = k_ref[pl.ds(off, tk), :]
        v = v_ref[pl.ds(off, tk), :]
        out = []
        for h in range(DSA_HEADS):
            m, l, acc = carry[h]
            s = _dot_nt(q_ref[:, h * HEAD_DIM:(h + 1) * HEAD_DIM], k) * scale
            s = jnp.where(sel, s, NEG_BIG)
            m_new = jnp.maximum(m, jnp.max(s, axis=-1, keepdims=True))
            p = jnp.exp(s - m_new)
            alpha = jnp.exp(m - m_new)
            l = alpha * l + jnp.sum(p, axis=-1, keepdims=True)
            acc = alpha * acc + _dot(p.astype(BF16), v)
            out.append((m_new, l, acc))
        return tuple(out)

    init = tuple((jnp.full((tq, 1), NEG_BIG, F32), jnp.zeros((tq, 1), F32),
                  jnp.zeros((tq, HEAD_DIM), F32)) for _ in range(DSA_HEADS))
    res = lax.fori_loop(0, nk, att, init)
    for h in range(DSA_HEADS):
        m, l, acc = res[h]
        o_ref[:, h * HEAD_DIM:(h + 1) * HEAD_DIM] = (acc / l).astype(o_ref.dtype)


def _dsa(dq, qi, wi, dk, dv, ki, topk, *, tq=256, tk=256):
    B, S, W = dq.shape
    return pl.pallas_call(
        functools.partial(_dsa_kernel, tq=tq, tk=tk, topk=topk, scale=HEAD_DIM ** -0.5),
        grid=(B, S // tq),
        in_specs=[
            pl.BlockSpec((None, tq, W), lambda b, i: (b, i, 0)),
            pl.BlockSpec((None, tq, IDX_W), lambda b, i: (b, i, 0)),
            pl.BlockSpec((None, tq, LANES), lambda b, i: (b, i, 0)),
            pl.BlockSpec((None, S, HEAD_DIM), lambda b, i: (b, 0, 0)),
            pl.BlockSpec((None, S, HEAD_DIM), lambda b, i: (b, 0, 0)),
            pl.BlockSpec((None, S, LANES), lambda b, i: (b, 0, 0)),
        ],
        out_specs=pl.BlockSpec((None, tq, W), lambda b, i: (b, i, 0)),
        out_shape=jax.ShapeDtypeStruct((B, S, W), BF16),
        scratch_shapes=[pltpu.VMEM((tq, S), I32)],
        compiler_params=_cparams(("parallel", "arbitrary")),
        name="dsa",
    )(dq, qi, wi, dk, dv, ki)


def _out_proj_kernel(x_ref, a_ref, s_ref, d_ref, w_ref, o_ref):
    acc = _dot(a_ref[...], w_ref[0:FOX_W, :])
    acc = acc + _dot(s_ref[...], w_ref[FOX_W:FOX_W + SSM_D_INNER, :])
    acc = acc + _dot(d_ref[...], w_ref[FOX_W + SSM_D_INNER:, :])
    o_ref[...] = x_ref[...] + acc


def _out_proj(x, fox_o, ssm_o, dsa_o, w, *, tm=512):
    T, D = x.shape
    blk = lambda wd: pl.BlockSpec((tm, wd), lambda i: (i, 0))
    return pl.pallas_call(
        _out_proj_kernel,
        grid=(T // tm,),
        in_specs=[blk(D), blk(FOX_W), blk(SSM_D_INNER), blk(DSA_W),
                  pl.BlockSpec(w.shape, lambda i: (0, 0))],
        out_specs=blk(D),
        out_shape=jax.ShapeDtypeStruct((T, D), F32),
        compiler_params=_cparams(("parallel",)),
        name="out_proj",
    )(x, fox_o, ssm_o, dsa_o, w)


def _relayout_w_in(w_in):
    sizes = (FOX_W, FOX_W, FOX_W, FOX_HEADS, SSM_D_INNER, SSM_CONV_CH, SSM_HEADS,
             DSA_Q_LORA, HEAD_DIM, HEAD_DIM, IDX_HEAD_DIM, IDX_HEADS)
    offs = [0]
    for s in sizes:
        offs.append(offs[-1] + s)
    seg = lambda n: w_in[:, :, offs[n]:offs[n + 1]]
    fq, fk, fv, fg, z, xbc, dt, cq, dk, dv, dki, dwi = (seg(n) for n in range(len(sizes)))
    zeros = lambda n: jnp.zeros(w_in.shape[:2] + (n,), w_in.dtype)
    small = jnp.concatenate([dki, fg, dt, dwi, zeros(LANES - SM_WI - IDX_HEADS)], axis=-1)
    tail = jnp.concatenate([cq, dk, dv, small, zeros(TAIL_W - T_SM - LANES)], axis=-1)
    return jnp.concatenate([xbc, fq, fk, fv, z, tail], axis=-1).astype(BF16)


def _rope_tables(S, head_dim):
    rot = head_dim // ROT_FRACTION
    half = rot // 2
    inv = ROPE_THETA ** (-jnp.arange(0, rot, 2, dtype=F32) / rot)
    ang = jnp.arange(S, dtype=F32)[:, None] * inv[None, :]
    cos, sin = jnp.cos(ang), jnp.sin(ang)
    ones = jnp.ones((S, head_dim - rot), F32)
    zer = jnp.zeros((S, head_dim - half), F32)
    c = jnp.concatenate([cos, cos, ones], axis=-1)
    s_lo = jnp.concatenate([-sin, zer], axis=-1)
    s_hi = jnp.concatenate([jnp.zeros((S, half), F32), sin, jnp.zeros((S, head_dim - rot), F32)], axis=-1)
    rep = LANES // head_dim
    return tuple(jnp.tile(t, (1, rep)) for t in (c, s_lo, s_hi))


def _lane_row(v, start):
    return jnp.zeros((1, LANES), F32).at[0, start:start + v.shape[0]].set(v.astype(F32))


def kernel(x, ffn1_norm, ffn1_w_gate, ffn1_w_up, ffn1_w_down, mix_norm, w_in, fox_fgate_b, fox_q_norm, fox_k_norm, ssm_conv_w, ssm_conv_b, ssm_dt_bias, ssm_a_log, ssm_d, ssm_norm, dsa_cq_norm, dsa_w_uq, dsa_w_uq_idx, dsa_q_norm, dsa_k_norm, dsa_kidx_norm, w_out, ffn2_norm, ffn2_w_gate, ffn2_w_up, ffn2_w_down):
    B, S, D = x.shape
    assert D == D_MODEL and S % 256 == 0
    depth = w_in.shape[0]
    T = B * S
    topk = min(TOPK_MAX, S // 4)

    w_in_p = _relayout_w_in(w_in)
    bf = lambda a: a.astype(BF16)
    f1g, f1u, f1d = bf(ffn1_w_gate), bf(ffn1_w_up), bf(ffn1_w_down)
    f2g, f2u, f2d = bf(ffn2_w_gate), bf(ffn2_w_up), bf(ffn2_w_down)
    wuq, wo = bf(dsa_w_uq), bf(w_out)
    wuqi = dsa_w_uq_idx.reshape(depth, DSA_Q_LORA, IDX_HEADS, IDX_HEAD_DIM)
    wuqi = bf(jnp.pad(wuqi, ((0, 0), (0, 0), (0, 0), (0, LANES - IDX_HEAD_DIM))).reshape(depth, DSA_Q_LORA, IDX_W))
    tab_a = _rope_tables(S, HEAD_DIM)
    tab_i = _rope_tables(S, IDX_HEAD_DIM)

    xt = x.reshape(T, D)
    for l in range(depth):
        xt = _ffn(xt, ffn1_norm[l][None], f1g[l], f1u[l], f1d[l])

        proj = _in_proj(xt, mix_norm[l][None], w_in_p[l])
        prep_params = (
            fox_q_norm[l][None], fox_k_norm[l][None], _lane_row(fox_fgate_b[l], SM_FG),
            dsa_cq_norm[l][None], wuq[l], wuqi[l],
            dsa_q_norm[l][None], dsa_k_norm[l][None], _lane_row(dsa_kidx_norm[l], SM_KI),
        )
        fq, fk, fv, sm, dq, qi, dk, dv, ki = _prep(proj, prep_params, tab_a + tab_i, S)

        r3 = lambda a: a.reshape(B, S, a.shape[-1])
        cum = _cumsum(r3(sm))
        c_row = jnp.swapaxes(cum[:, :, SM_FG:SM_FG + FOX_HEADS], 1, 2)
        fox_o = _fox(r3(fq), r3(fk), r3(fv), cum, c_row)

        dtb_l = _lane_row(ssm_dt_bias[l], SM_DT)
        alog_l = _lane_row(ssm_a_log[l], SM_DT)
        ssd_params = (
            ssm_conv_w[l], ssm_conv_b[l][None], dtb_l, alog_l,
            jnp.broadcast_to(dtb_l.T, (LANES, LANES)), jnp.broadcast_to(alog_l.T, (LANES, LANES)),
            jnp.repeat(ssm_d[l], SSM_HEAD_DIM)[None], ssm_norm[l][None],
        )
        ssm_o = _ssd(r3(proj), ssd_params)

        dsa_o = _dsa(r3(dq), r3(qi), r3(sm), r3(dk), r3(dv), r3(ki), topk)

        xt = _out_proj(xt, fox_o.reshape(T, FOX_W), ssm_o.reshape(T, SSM_D_INNER),
                       dsa_o.reshape(T, DSA_W), wo[l])

        xt = _ffn(xt, ffn2_norm[l][None], f2g[l], f2u[l], f2d[l])
    return xt.reshape(B, S, D)
```

```python
import functools
import math

import jax
import jax.numpy as jnp
from jax import lax
from jax.experimental import pallas as pl
from jax.experimental.pallas import tpu as pltpu

F32 = jnp.float32
BF16 = jnp.bfloat16
I32 = jnp.int32

D_MODEL = 2048
D_FF = 2 * D_MODEL
NORM_EPS = 1e-6
GATED_NORM_EPS = 1e-5
ROPE_THETA = 500000.0
ROT_FRACTION = 4
HEAD_DIM = 128
FOX_HEADS = 4
SSM_D_INNER = D_MODEL // 2
SSM_HEAD_DIM = 64
SSM_HEADS = SSM_D_INNER // SSM_HEAD_DIM
SSM_GROUPS = 2
SSM_STATE = 128
SSM_CONV = 4
SSM_CHUNK = 128
SSM_CONV_CH = SSM_D_INNER + 2 * SSM_GROUPS * SSM_STATE
DSA_HEADS = 4
DSA_Q_LORA = 512
IDX_HEADS = 8
IDX_HEAD_DIM = 64
TOPK_MAX = 256
FOX_W = FOX_HEADS * HEAD_DIM
DSA_W = DSA_HEADS * HEAD_DIM
IDX_W = IDX_HEADS * 128

LANES = 128
SUBLANES = 8
V7X_VMEM_BUDGET = 56 * 1024 * 1024

P_XBC = 0
P_FQKV = P_XBC + SSM_CONV_CH
P_Z = P_FQKV + 3 * FOX_W
P_TAIL = P_Z + SSM_D_INNER
T_CQ = 0
T_DK = T_CQ + DSA_Q_LORA
T_DV = T_DK + HEAD_DIM
T_SM = T_DV + HEAD_DIM
TAIL_W = 1024
NP = P_TAIL + TAIL_W
SM_KI = 0
SM_FG = IDX_HEAD_DIM
SM_DT = SM_FG + FOX_HEADS
SM_WI = SM_DT + SSM_HEADS

NEG_BIG = -1e30
INT_MIN = -2 ** 31
LOG2E = math.log2(math.e)
QK_SCALE_LOG2 = HEAD_DIM ** -0.5 * LOG2E


def _cparams(sem, vmem=V7X_VMEM_BUDGET):
    return pltpu.CompilerParams(dimension_semantics=sem, vmem_limit_bytes=vmem)


def _sigmoid(x):
    return 1.0 / (1.0 + jnp.exp(-x))


def _softplus(x):
    return jnp.maximum(x, 0.0) + jnp.log(1.0 + jnp.exp(-jnp.abs(x)))


def _rms(x, eps):
    return x * lax.rsqrt(jnp.mean(x * x, axis=-1, keepdims=True) + eps)


def _dot_nt(a, b):
    return lax.dot_general(a, b, (((1,), (1,)), ((), ())), preferred_element_type=F32)


def _dot(a, b):
    return jnp.dot(a, b, preferred_element_type=F32)


def _dot_f32(a, b):
    return jnp.dot(a, b, preferred_element_type=F32, precision=lax.Precision.HIGHEST)


def _ffn_kernel(x_ref, g_ref, wg_ref, wu_ref, wd_ref, o_ref, h_ref, *, nj):
    j = pl.program_id(1)

    @pl.when(j == 0)
    def _():
        h_ref[...] = (_rms(x_ref[...], NORM_EPS) * g_ref[...]).astype(BF16)
        o_ref[...] = jnp.zeros_like(o_ref)

    h = h_ref[...]
    gate = _dot(h, wg_ref[...])
    up = _dot(h, wu_ref[...])
    a = (gate * _sigmoid(gate) * up).astype(BF16)
    o_ref[...] += _dot(a, wd_ref[...])

    @pl.when(j == nj - 1)
    def _():
        o_ref[...] = x_ref[...] + 0.5 * o_ref[...]


def _ffn(x, g, wg, wu, wd, layer, *, tm=1024, tf=512):
    T, D = x.shape
    F = wg.shape[2]
    nj = F // tf
    return pl.pallas_call(
        functools.partial(_ffn_kernel, nj=nj),
        grid=(T // tm, nj),
        in_specs=[
            pl.BlockSpec((tm, D), lambda i, j: (i, 0)),
            pl.BlockSpec((1, D), lambda i, j: (0, 0)),
            pl.BlockSpec((None, D, tf), lambda i, j: (layer, 0, j)),
            pl.BlockSpec((None, D, tf), lambda i, j: (layer, 0, j)),
            pl.BlockSpec((None, tf, D), lambda i, j: (layer, j, 0)),
        ],
        out_specs=pl.BlockSpec((tm, D), lambda i, j: (i, 0)),
        out_shape=jax.ShapeDtypeStruct((T, D), F32),
        scratch_shapes=[pltpu.VMEM((tm, D), BF16)],
        compiler_params=_cparams(("parallel", "arbitrary")),
        name="ffn",
    )(x, g, wg, wu, wd)


def _in_proj_kernel(x_ref, g_ref, w_ref, o_ref, h_ref):
    @pl.when(pl.program_id(1) == 0)
    def _():
        h_ref[...] = (_rms(x_ref[...], NORM_EPS) * g_ref[...]).astype(BF16)

    o_ref[...] = _dot(h_ref[...], w_ref[...])


def _in_proj(x, g, w, layer, *, tm=1024, tn=1024):
    T, D = x.shape
    N = w.shape[2]
    return pl.pallas_call(
        _in_proj_kernel,
        grid=(T // tm, N // tn),
        in_specs=[
            pl.BlockSpec((tm, D), lambda i, j: (i, 0)),
            pl.BlockSpec((1, D), lambda i, j: (0, 0)),
            pl.BlockSpec((None, D, tn), lambda i, j: (layer, 0, j)),
        ],
        out_specs=pl.BlockSpec((tm, tn), lambda i, j: (i, j)),
        out_shape=jax.ShapeDtypeStruct((T, N), F32),
        scratch_shapes=[pltpu.VMEM((tm, D), BF16)],
        compiler_params=_cparams(("parallel", "arbitrary")),
        name="in_proj",
    )(x, g, w)


def _rope(x, c, s_lo, s_hi, half):
    n = x.shape[-1]
    return x * c + pltpu.roll(x, n - half, 1) * s_lo + pltpu.roll(x, half, 1) * s_hi


def _prep_kernel(fqkv_ref, tail_ref, fqg_ref, fkg_ref, fgb_ref, cqg_ref, wuq_ref, wuqi_ref,
                 dqg_ref, dkg_ref, kig_ref, ca_ref, sa1_ref, sa2_ref, ci_ref, si1_ref, si2_ref,
                 fq_ref, fk_ref, fv_ref, sm_ref, dq_ref, qi_ref, dk_ref, dv_ref, ki_ref, *, w_scale):
    for h in range(FOX_HEADS):
        lo, hi = h * HEAD_DIM, (h + 1) * HEAD_DIM
        q = fqkv_ref[:, lo:hi]
        k = fqkv_ref[:, FOX_W + lo:FOX_W + hi]
        fq_ref[:, lo:hi] = (_rms(q, NORM_EPS) * fqg_ref[...] * QK_SCALE_LOG2).astype(BF16)
        fk_ref[:, lo:hi] = (_rms(k, NORM_EPS) * fkg_ref[...]).astype(BF16)
    fv_ref[...] = fqkv_ref[:, 2 * FOX_W:3 * FOX_W].astype(BF16)

    sm = tail_ref[:, T_SM:T_SM + LANES]
    lane = lax.broadcasted_iota(I32, sm.shape, 1)
    zf = sm + fgb_ref[...]
    logf = jnp.minimum(zf, 0.0) - jnp.log(1.0 + jnp.exp(-jnp.abs(zf)))
    is_fg = (lane >= SM_FG) & (lane < SM_FG + FOX_HEADS)
    sm_ref[...] = jnp.where(is_fg, logf, sm * w_scale)

    is_ki = lane < IDX_HEAD_DIM
    ms = jnp.sum(jnp.where(is_ki, sm * sm, 0.0), axis=-1, keepdims=True) * (1.0 / IDX_HEAD_DIM)
    kin = sm * lax.rsqrt(ms + NORM_EPS) * kig_ref[...]
    kin = _rope(kin, ci_ref[...], si1_ref[...], si2_ref[...], IDX_HEAD_DIM // ROT_FRACTION // 2)
    ki_ref[...] = kin.astype(BF16)

    dk = _rms(tail_ref[:, T_DK:T_DK + HEAD_DIM], NORM_EPS) * dkg_ref[...]
    dk_ref[...] = _rope(dk, ca_ref[...], sa1_ref[...], sa2_ref[...], HEAD_DIM // ROT_FRACTION // 2).astype(BF16)
    dv_ref[...] = tail_ref[:, T_DV:T_DV + HEAD_DIM].astype(BF16)

    cq = (_rms(tail_ref[:, T_CQ:T_CQ + DSA_Q_LORA], NORM_EPS) * cqg_ref[...]).astype(BF16)
    uq = _dot(cq, wuq_ref[...])
    uqi = _dot(cq, wuqi_ref[...])
    for h in range(DSA_HEADS):
        lo, hi = h * HEAD_DIM, (h + 1) * HEAD_DIM
        qh = _rms(uq[:, lo:hi], NORM_EPS) * dqg_ref[...]
        qh = _rope(qh, ca_ref[...], sa1_ref[...], sa2_ref[...], HEAD_DIM // ROT_FRACTION // 2)
        dq_ref[:, lo:hi] = (qh * QK_SCALE_LOG2).astype(BF16)
    for p in range(IDX_W // LANES):
        lo, hi = p * LANES, (p + 1) * LANES
        qi_ref[:, lo:hi] = _rope(uqi[:, lo:hi], ci_ref[...], si1_ref[...], si2_ref[...],
                                 IDX_HEAD_DIM // ROT_FRACTION // 2).astype(BF16)


def _prep(proj, params, wuq, wuqi, layer, tables, S, *, tm=256):
    T = proj.shape[0]
    ns = S // tm
    w_scale = (IDX_HEADS ** -0.5) * (IDX_HEAD_DIM ** -0.5)
    row = lambda w: pl.BlockSpec((1, w), lambda i: (0, 0))
    tab = pl.BlockSpec((tm, LANES), lambda i: (i % ns, 0))
    out_bf = lambda w: jax.ShapeDtypeStruct((T, w), BF16)
    out_blk = lambda w: pl.BlockSpec((tm, w), lambda i: (i, 0))
    fqg, fkg, fgb, cqg, dqg, dkg, kig = params
    return pl.pallas_call(
        functools.partial(_prep_kernel, w_scale=w_scale),
        grid=(T // tm,),
        in_specs=[
            pl.BlockSpec((tm, 3 * FOX_W), lambda i: (i, P_FQKV // (3 * FOX_W))),
            pl.BlockSpec((tm, TAIL_W), lambda i: (i, P_TAIL // TAIL_W)),
            row(LANES), row(LANES), row(LANES), row(DSA_Q_LORA),
            pl.BlockSpec((None, DSA_Q_LORA, DSA_W), lambda i: (layer, 0, 0)),
            pl.BlockSpec((None, DSA_Q_LORA, IDX_W), lambda i: (layer, 0, 0)),
            row(LANES), row(LANES), row(LANES),
            tab, tab, tab, tab, tab, tab,
        ],
        out_specs=[out_blk(FOX_W), out_blk(FOX_W), out_blk(FOX_W), out_blk(LANES),
                   out_blk(DSA_W), out_blk(IDX_W), out_blk(HEAD_DIM), out_blk(HEAD_DIM),
                   out_blk(LANES)],
        out_shape=[out_bf(FOX_W), out_bf(FOX_W), out_bf(FOX_W),
                   jax.ShapeDtypeStruct((T, LANES), F32),
                   out_bf(DSA_W), out_bf(IDX_W), out_bf(HEAD_DIM), out_bf(HEAD_DIM),
                   out_bf(LANES)],
        compiler_params=_cparams(("parallel",)),
        name="prep",
    )(proj, proj, fqg, fkg, fgb, cqg, wuq, wuqi, dqg, dkg, kig, *tables)


def _cumsum_kernel(x_ref, o_ref, *, ch):
    S = x_ref.shape[0]
    r = lax.broadcasted_iota(I32, (ch, ch), 0)
    c = lax.broadcasted_iota(I32, (ch, ch), 1)
    tri = jnp.where(c <= r, 1.0, 0.0).astype(F32)

    def body(i, carry):
        off = pl.multiple_of(i * ch, ch)
        cs = _dot_f32(tri, x_ref[pl.ds(off, ch), :]) + carry
        o_ref[pl.ds(off, ch), :] = cs
        return cs[ch - 1:ch, :]

    lax.fori_loop(0, S // ch, body, jnp.zeros((1, x_ref.shape[1]), F32))


def _cumsum(x3, *, ch=128):
    B, S, W = x3.shape
    return pl.pallas_call(
        functools.partial(_cumsum_kernel, ch=ch),
        grid=(B,),
        in_specs=[pl.BlockSpec((None, S, W), lambda b: (b, 0, 0))],
        out_specs=pl.BlockSpec((None, S, W), lambda b: (b, 0, 0)),
        out_shape=jax.ShapeDtypeStruct((B, S, W), F32),
        compiler_params=_cparams(("parallel",)),
        name="cumsum",
    )(x3)


def _softmax_attend(logits, values, n_plain, tail, t, s_ref, mx_ref, l_ref, acc_ref):
    ng = t // LANES

    def pass_a(j, is_tail):
        off = pl.multiple_of(j * t, t)
        s = logits(j, is_tail)
        s_ref[:, pl.ds(off, t)] = s
        part = s[:, 0:LANES]
        for g in range(1, ng):
            part = jnp.maximum(part, s[:, g * LANES:(g + 1) * LANES])
        mx_ref[...] = jnp.maximum(mx_ref[...], part)

    def pass_b(j):
        off = pl.multiple_of(j * t, t)
        ps = [jnp.exp2(s_ref[:, pl.ds(off + g * LANES, LANES)] - mx_ref[...]) for g in range(ng)]
        part = ps[0]
        for g in range(1, ng):
            part = part + ps[g]
        l_ref[...] += part
        acc_ref[...] += _dot(jnp.concatenate(ps, axis=1).astype(BF16), values(j))

    def loop(fn):
        def body(jj, carry):
            fn(2 * jj)
            fn(2 * jj + 1)
            return carry
        lax.fori_loop(0, lax.shift_right_logical(n_plain, 1), body, 0)

        @pl.when((n_plain & 1) == 1)
        def _():
            fn(n_plain - 1)

    mx_ref[...] = jnp.full(mx_ref.shape, -jnp.inf, F32)
    loop(lambda j: pass_a(j, False))
    if tail is not None:
        pass_a(tail, True)
    mx_ref[...] = jnp.broadcast_to(jnp.max(mx_ref[...], axis=-1, keepdims=True), mx_ref.shape)
    l_ref[...] = jnp.zeros_like(l_ref)
    acc_ref[...] = jnp.zeros_like(acc_ref)
    loop(pass_b)
    if tail is not None:
        pass_b(tail)
    return acc_ref[...] / jnp.sum(l_ref[...], axis=-1, keepdims=True)


def _fox_kernel(q_ref, k_ref, v_ref, cr_ref, o_ref, s_ref, mx_ref, l_ref, acc_ref, *, t):
    i = pl.program_id(1)
    row = lax.broadcasted_iota(I32, (t, t), 0)
    col = lax.broadcasted_iota(I32, (t, t), 1)
    for h in range(FOX_HEADS):
        lo, hi = h * HEAD_DIM, (h + 1) * HEAD_DIM

        def logits(j, diagonal, h=h, lo=lo, hi=hi):
            off = pl.multiple_of(j * t, t)
            s = _dot_nt(q_ref[:, lo:hi], k_ref[pl.ds(off, t), lo:hi])
            s = s + cr_ref[h:h + 1, pl.ds(off, t)] * (-LOG2E)
            return jnp.where(col <= row, s, -jnp.inf) if diagonal else s

        def values(j, lo=lo, hi=hi):
            return v_ref[pl.ds(pl.multiple_of(j * t, t), t), lo:hi]

        out = _softmax_attend(logits, values, i, i, t, s_ref, mx_ref, l_ref, acc_ref)
        o_ref[:, lo:hi] = out.astype(o_ref.dtype)


def _fox(q, k, v, c_row, *, t=512):
    B, S, W = q.shape
    return pl.pallas_call(
        functools.partial(_fox_kernel, t=t),
        grid=(B, S // t),
        in_specs=[
            pl.BlockSpec((None, t, W), lambda b, i: (b, i, 0)),
            pl.BlockSpec((None, S, W), lambda b, i: (b, 0, 0)),
            pl.BlockSpec((None, S, W), lambda b, i: (b, 0, 0)),
            pl.BlockSpec((None, FOX_HEADS, S), lambda b, i: (b, 0, 0)),
        ],
        out_specs=pl.BlockSpec((None, t, W), lambda b, i: (b, i, 0)),
        out_shape=jax.ShapeDtypeStruct((B, S, W), BF16),
        scratch_shapes=[
            pltpu.VMEM((t, S), F32),
            pltpu.VMEM((t, LANES), F32),
            pltpu.VMEM((t, LANES), F32),
            pltpu.VMEM((t, HEAD_DIM), F32),
        ],
        compiler_params=_cparams(("parallel", "arbitrary")),
        name="fox",
    )(q, k, v, c_row)


def _ssd_kernel(xbc_ref, z_ref, sm_ref, cw_ref, cb_ref, dtbl_ref, alogl_ref, dtbr_ref, alogr_ref,
                dsk_ref, ng_ref, o_ref, xpad, state, ybuf):
    c = pl.program_id(1)
    L = SSM_CHUNK
    PAD = SUBLANES

    @pl.when(c == 0)
    def _():
        xpad[0:PAD, :] = jnp.zeros((PAD, SSM_CONV_CH), F32)
        state[...] = jnp.zeros_like(state)

    @pl.when(c > 0)
    def _():
        xpad[0:PAD, :] = xpad[L:L + PAD, :]

    xpad[PAD:PAD + L, :] = xbc_ref[...]
    conv = cb_ref[...] + cw_ref[0:1, :] * xpad[PAD - 3:PAD - 3 + L, :]
    for kk in range(1, SSM_CONV):
        conv = conv + cw_ref[kk:kk + 1, :] * xpad[PAD - 3 + kk:PAD - 3 + kk + L, :]
    xc = conv * _sigmoid(conv)
    gn = SSM_GROUPS * SSM_STATE

    sm = sm_ref[...]
    dt_c = _softplus(sm + dtbl_ref[...])
    a_c = dt_c * (-jnp.exp(alogl_ref[...]))
    dt_r = _softplus(sm.T + dtbr_ref[...])
    a_r = dt_r * (-jnp.exp(alogr_ref[...]))
    ri = lax.broadcasted_iota(I32, (L, L), 0)
    ci = lax.broadcasted_iota(I32, (L, L), 1)
    tril = ci <= ri
    acs_c = _dot_f32(jnp.where(tril, 1.0, 0.0).astype(F32), a_c)
    acs_r = _dot_f32(a_r, jnp.where(ri <= ci, 1.0, 0.0).astype(F32))

    for g in range(SSM_GROUPS):
        bg = xc[:, SSM_D_INNER + g * SSM_STATE:SSM_D_INNER + (g + 1) * SSM_STATE]
        cg = xc[:, SSM_D_INNER + gn + g * SSM_STATE:SSM_D_INNER + gn + (g + 1) * SSM_STATE]
        cg_b = cg.astype(BF16)
        gmat = _dot_nt(cg_b, bg.astype(BF16))
        bgt = bg.T
        for hh in range(SSM_HEADS // SSM_GROUPS):
            h = g * (SSM_HEADS // SSM_GROUPS) + hh
            lo, hi = h * SSM_HEAD_DIM, (h + 1) * SSM_HEAD_DIM
            colv = acs_c[:, SM_DT + h:SM_DT + h + 1]
            rowv = acs_r[SM_DT + h:SM_DT + h + 1, :]
            last = rowv[:, L - 1:L]
            seg = jnp.exp(jnp.where(tril, colv - rowv, -jnp.inf))
            x_h = xc[:, lo:hi]
            xdt = (x_h * dt_c[:, SM_DT + h:SM_DT + h + 1]).astype(BF16)
            st = state[h]
            y = _dot((gmat * seg).astype(BF16), xdt) + _dot(cg_b, st.astype(BF16)) * jnp.exp(colv)
            dec = jnp.exp(last - rowv)
            state[h] = st * jnp.exp(last) + _dot((bgt * dec).astype(BF16), xdt)
            ybuf[:, lo:hi] = y + x_h * dsk_ref[:, lo:hi]

    zz = z_ref[...]
    u = ybuf[...] * (zz * _sigmoid(zz))
    gw = SSM_D_INNER // SSM_GROUPS
    for g in range(SSM_GROUPS):
        ug = _rms(u[:, g * gw:(g + 1) * gw], GATED_NORM_EPS)
        o_ref[:, g * gw:(g + 1) * gw] = (ug * ng_ref[:, g * gw:(g + 1) * gw]).astype(o_ref.dtype)


def _ssd(proj3, params):
    B, S, _ = proj3.shape
    L = SSM_CHUNK
    full = lambda a: pl.BlockSpec(a.shape, lambda b, c: (0,) * a.ndim)
    return pl.pallas_call(
        _ssd_kernel,
        grid=(B, S // L),
        in_specs=[
            pl.BlockSpec((None, L, SSM_CONV_CH), lambda b, c: (b, c, P_XBC // SSM_CONV_CH)),
            pl.BlockSpec((None, L, SSM_D_INNER), lambda b, c: (b, c, P_Z // SSM_D_INNER)),
            pl.BlockSpec((None, L, LANES), lambda b, c: (b, c, (P_TAIL + T_SM) // LANES)),
        ] + [full(a) for a in params],
        out_specs=pl.BlockSpec((None, L, SSM_D_INNER), lambda b, c: (b, c, 0)),
        out_shape=jax.ShapeDtypeStruct((B, S, SSM_D_INNER), BF16),
        scratch_shapes=[
            pltpu.VMEM((L + SUBLANES, SSM_CONV_CH), F32),
            pltpu.VMEM((SSM_HEADS, SSM_STATE, SSM_HEAD_DIM), F32),
            pltpu.VMEM((L, SSM_D_INNER), F32),
        ],
        compiler_params=_cparams(("parallel", "arbitrary")),
        name="ssd",
    )(proj3, proj3, proj3, *params)


def _dsa_kernel(q_ref, qi_ref, wi_ref, k_ref, v_ref, ki_ref, o_ref,
                key_ref, bias_ref, s_ref, k16_ref, pref_ref, cge_ref, cand_ref, cnt_ref, tie_ref,
                cand16_ref, cnt16_ref, mx_ref, l_ref, acc_ref, *, t, topk):
    i = pl.program_id(1)
    nk = i + 1
    ng = t // LANES
    imin = jnp.int32(INT_MIN)
    ones = jnp.ones((LANES, LANES), BF16)
    row = lax.broadcasted_iota(I32, (t, t), 0)
    col = lax.broadcasted_iota(I32, (t, t), 1)

    def scores(j, diagonal):
        off = pl.multiple_of(j * t, t)
        kit = ki_ref[pl.ds(off, t), :]
        acc = jnp.zeros((t, t), F32)
        for h in range(IDX_HEADS):
            s = _dot_nt(qi_ref[:, h * LANES:(h + 1) * LANES], kit)
            acc = acc + wi_ref[:, SM_WI + h:SM_WI + h + 1] * jnp.maximum(s, 0.0)
        acc = jnp.where(acc == 0.0, 0.0, acc)
        bits = pltpu.bitcast(acc, I32)
        key = bits ^ ((bits >> 31) & jnp.int32(0x7FFFFFFF))
        if diagonal:
            key = jnp.where(col <= row, key, imin)
        key_ref[:, pl.ds(off, t)] = key

    def scores_body(j, carry):
        scores(j, False)
        return carry

    lax.fori_loop(0, i, scores_body, 0)
    scores(i, True)

    def count(pred):
        cnt_ref[...] = jnp.zeros_like(cnt_ref)

        def body(j, carry):
            off = pl.multiple_of(j * t, t)
            part = jnp.where(pred(key_ref[:, pl.ds(off, LANES)]), 1.0, 0.0)
            for g in range(1, ng):
                part = part + jnp.where(pred(key_ref[:, pl.ds(off + g * LANES, LANES)]), 1.0, 0.0)
            cnt_ref[...] += part
            return carry

        lax.fori_loop(0, nk, body, 0)
        return _dot(cnt_ref[...].astype(BF16), ones)

    kf = float(topk)
    half = 1 << 15
    one16, zero16 = jnp.ones((), BF16), jnp.zeros((), BF16)

    def count16(pred):
        cnt16_ref[...] = jnp.zeros_like(cnt16_ref)

        def body(j, carry):
            off = pl.multiple_of(j * t, t)
            part = jnp.where(pred(k16_ref[:, pl.ds(off, LANES)]), one16, zero16)
            for g in range(1, ng):
                part = part + jnp.where(pred(k16_ref[:, pl.ds(off + g * LANES, LANES)]), one16, zero16)
            cnt16_ref[...] += part
            return carry

        lax.fori_loop(0, nk, body, 0)
        return _dot(cnt16_ref[...], ones)

    def search16(kneed):
        pref_ref[...] = jnp.zeros_like(pref_ref)
        cge_ref[...] = jnp.zeros_like(cge_ref)

        def step(b, carry):
            cand_u = pref_ref[...] | lax.shift_left(jnp.int32(1), 15 - b)
            cand16_ref[...] = (cand_u - half).astype(jnp.int16)
            cnt = count16(lambda kt: kt >= cand16_ref[...])
            ok = cnt >= kneed
            pref_ref[...] = jnp.where(ok, cand_u, pref_ref[...])
            cge_ref[...] = jnp.where(ok, cnt, cge_ref[...])
            return carry

        lax.fori_loop(0, 16, step, 0)
        return pref_ref[...], cge_ref[...]

    def fill_hi(jj, carry):
        off = pl.multiple_of(jj * LANES, LANES)
        k16_ref[:, pl.ds(off, LANES)] = (key_ref[:, pl.ds(off, LANES)] >> 16).astype(jnp.int16)
        return carry

    lax.fori_loop(0, nk * ng, fill_hi, 0)
    hi_u, cge_hi = search16(kf)
    hi_s = hi_u - half
    cand16_ref[...] = hi_s.astype(jnp.int16)
    need_lo = kf - count16(lambda kt: kt > cand16_ref[...])

    cand_ref[...] = hi_s

    def fill_lo(jj, carry):
        off = pl.multiple_of(jj * LANES, LANES)
        key = key_ref[:, pl.ds(off, LANES)]
        lo_s = (key & jnp.int32(0xFFFF)) - half
        k16_ref[:, pl.ds(off, LANES)] = jnp.where((key >> 16) == cand_ref[...], lo_s, -half).astype(jnp.int16)
        return carry

    lax.fori_loop(0, nk * ng, fill_lo, 0)
    lo_u, cge_lo = search16(need_lo)
    cand_ref[...] = hi_s * (1 << 16) + lo_u

    c_ge = jnp.where(lo_u != 0, (kf - need_lo) + cge_lo, cge_hi)
    tie_row = ((hi_u != 0) | (lo_u != 0)) & (c_ge > kf)
    tie_ref[...] = jnp.where(tie_row, 1.0, 0.0)
    n_tie = jnp.sum(tie_ref[...])

    @pl.when(n_tie > 0.0)
    def _():
        need = kf - count(lambda kt: kt > cand_ref[...])
        rr = lax.broadcasted_iota(I32, (LANES, LANES), 0)
        cc = lax.broadcasted_iota(I32, (LANES, LANES), 1)
        upper = jnp.where(rr < cc, 1.0, 0.0).astype(BF16)
        tied = tie_ref[...] > 0.0
        cnt_ref[...] = jnp.zeros_like(cnt_ref)

        def fix(jj, carry):
            off = pl.multiple_of(jj * LANES, LANES)
            kt = key_ref[:, pl.ds(off, LANES)]
            eq = (kt == cand_ref[...]) & tied
            eqb = jnp.where(eq, 1.0, 0.0).astype(BF16)
            rank = cnt_ref[...] + _dot(eqb, upper)
            drop = eq & (rank >= need)
            key_ref[:, pl.ds(off, LANES)] = jnp.where(drop, imin, kt)
            cnt_ref[...] += _dot(eqb, ones)
            return carry

        lax.fori_loop(0, nk * ng, fix, 0)

    cand_ref[...] = jnp.maximum(cand_ref[...], imin + 1)

    def fill_bias(jj, carry):
        off = pl.multiple_of(jj * LANES, LANES)
        sel = key_ref[:, pl.ds(off, LANES)] >= cand_ref[...]
        bias_ref[:, pl.ds(off, LANES)] = jnp.where(sel, 0.0, NEG_BIG)
        return carry

    lax.fori_loop(0, nk * ng, fill_bias, 0)

    for h in range(DSA_HEADS):
        lo, hi = h * HEAD_DIM, (h + 1) * HEAD_DIM

        def logits(j, is_tail, lo=lo, hi=hi):
            off = pl.multiple_of(j * t, t)
            return _dot_nt(q_ref[:, lo:hi], k_ref[pl.ds(off, t), :]) + bias_ref[:, pl.ds(off, t)]

        def values(j):
            return v_ref[pl.ds(pl.multiple_of(j * t, t), t), :]

        out = _softmax_attend(logits, values, nk, None, t, s_ref, mx_ref, l_ref, acc_ref)
        o_ref[:, lo:hi] = out.astype(o_ref.dtype)


def _dsa(dq, qi, wi, dk, dv, ki, topk, *, t=512):
    B, S, W = dq.shape
    rep = lambda dt: pltpu.VMEM((t, LANES), dt)
    return pl.pallas_call(
        functools.partial(_dsa_kernel, t=t, topk=topk),
        grid=(B, S // t),
        in_specs=[
            pl.BlockSpec((None, t, W), lambda b, i: (b, i, 0)),
            pl.BlockSpec((None, t, IDX_W), lambda b, i: (b, i, 0)),
            pl.BlockSpec((None, t, LANES), lambda b, i: (b, i, 0)),
            pl.BlockSpec((None, S, HEAD_DIM), lambda b, i: (b, 0, 0)),
            pl.BlockSpec((None, S, HEAD_DIM), lambda b, i: (b, 0, 0)),
            pl.BlockSpec((None, S, LANES), lambda b, i: (b, 0, 0)),
        ],
        out_specs=pl.BlockSpec((None, t, W), lambda b, i: (b, i, 0)),
        out_shape=jax.ShapeDtypeStruct((B, S, W), BF16),
        scratch_shapes=[
            pltpu.VMEM((t, S), I32),
            pltpu.VMEM((t, S), F32),
            pltpu.VMEM((t, S), F32),
            pltpu.VMEM((t, S), jnp.int16),
            rep(I32), rep(F32), rep(I32), rep(F32), rep(F32),
            rep(jnp.int16), rep(BF16),
            rep(F32), rep(F32),
            pltpu.VMEM((t, HEAD_DIM), F32),
        ],
        compiler_params=_cparams(("parallel", "arbitrary")),
        name="dsa",
    )(dq, qi, wi, dk, dv, ki)


def _out_proj_kernel(x_ref, a_ref, s_ref, d_ref, w_ref, o_ref):
    acc = _dot(a_ref[...], w_ref[0:FOX_W, :])
    acc = acc + _dot(s_ref[...], w_ref[FOX_W:FOX_W + SSM_D_INNER, :])
    acc = acc + _dot(d_ref[...], w_ref[FOX_W + SSM_D_INNER:, :])
    o_ref[...] = x_ref[...] + acc


def _out_proj(x, fox_o, ssm_o, dsa_o, w, layer, *, tm=512):
    T, D = x.shape
    blk = lambda wd: pl.BlockSpec((tm, wd), lambda i: (i, 0))
    return pl.pallas_call(
        _out_proj_kernel,
        grid=(T // tm,),
        in_specs=[blk(D), blk(FOX_W), blk(SSM_D_INNER), blk(DSA_W),
                  pl.BlockSpec((None,) + w.shape[1:], lambda i: (layer, 0, 0))],
        out_specs=blk(D),
        out_shape=jax.ShapeDtypeStruct((T, D), F32),
        compiler_params=_cparams(("parallel",)),
        name="out_proj",
    )(x, fox_o, ssm_o, dsa_o, w)


def _relayout_w_in(w_in):
    sizes = (FOX_W, FOX_W, FOX_W, FOX_HEADS, SSM_D_INNER, SSM_CONV_CH, SSM_HEADS,
             DSA_Q_LORA, HEAD_DIM, HEAD_DIM, IDX_HEAD_DIM, IDX_HEADS)
    offs = [0]
    for s in sizes:
        offs.append(offs[-1] + s)
    seg = lambda n: w_in[:, :, offs[n]:offs[n + 1]].astype(BF16)
    fq, fk, fv, fg, z, xbc, dt, cq, dk, dv, dki, dwi = (seg(n) for n in range(len(sizes)))
    zeros = lambda n: jnp.zeros(w_in.shape[:2] + (n,), BF16)
    small = jnp.concatenate([dki, fg, dt, dwi, zeros(LANES - SM_WI - IDX_HEADS)], axis=-1)
    tail = jnp.concatenate([cq, dk, dv, small, zeros(TAIL_W - T_SM - LANES)], axis=-1)
    return jnp.concatenate([xbc, fq, fk, fv, z, tail], axis=-1)


def _rope_tables(S, head_dim):
    rot = head_dim // ROT_FRACTION
    half = rot // 2
    inv = ROPE_THETA ** (-jnp.arange(0, rot, 2, dtype=F32) / rot)
    ang = jnp.arange(S, dtype=F32)[:, None] * inv[None, :]
    cos, sin = jnp.cos(ang), jnp.sin(ang)
    ones = jnp.ones((S, head_dim - rot), F32)
    zer = jnp.zeros((S, head_dim - half), F32)
    c = jnp.concatenate([cos, cos, ones], axis=-1)
    s_lo = jnp.concatenate([-sin, zer], axis=-1)
    s_hi = jnp.concatenate([jnp.zeros((S, half), F32), sin, jnp.zeros((S, head_dim - rot), F32)], axis=-1)
    rep = LANES // head_dim
    return tuple(jnp.tile(t, (1, rep)) for t in (c, s_lo, s_hi))


def _lane_row(v, start):
    return jnp.zeros((1, LANES), F32).at[0, start:start + v.shape[0]].set(v.astype(F32))


def kernel(x, ffn1_norm, ffn1_w_gate, ffn1_w_up, ffn1_w_down, mix_norm, w_in, fox_fgate_b, fox_q_norm, fox_k_norm, ssm_conv_w, ssm_conv_b, ssm_dt_bias, ssm_a_log, ssm_d, ssm_norm, dsa_cq_norm, dsa_w_uq, dsa_w_uq_idx, dsa_q_norm, dsa_k_norm, dsa_kidx_norm, w_out, ffn2_norm, ffn2_w_gate, ffn2_w_up, ffn2_w_down):
    B, S, D = x.shape
    assert D == D_MODEL and S % 512 == 0 and (B * S) % 1024 == 0
    depth = w_in.shape[0]
    T = B * S
    topk = min(TOPK_MAX, S // 4)

    w_in_p = _relayout_w_in(w_in)
    bf = lambda a: a.astype(BF16)
    f1g, f1u, f1d = bf(ffn1_w_gate), bf(ffn1_w_up), bf(ffn1_w_down)
    f2g, f2u, f2d = bf(ffn2_w_gate), bf(ffn2_w_up), bf(ffn2_w_down)
    wuq, wo = bf(dsa_w_uq), bf(w_out)
    wuqi = bf(dsa_w_uq_idx).reshape(depth, DSA_Q_LORA, IDX_HEADS, IDX_HEAD_DIM)
    wuqi = jnp.pad(wuqi, ((0, 0), (0, 0), (0, 0), (0, LANES - IDX_HEAD_DIM))).reshape(depth, DSA_Q_LORA, IDX_W)
    tab_a = _rope_tables(S, HEAD_DIM)
    tab_i = _rope_tables(S, IDX_HEAD_DIM)

    xt = x.reshape(T, D)
    for l in range(depth):
        xt = _ffn(xt, ffn1_norm[l][None], f1g, f1u, f1d, l)

        proj = _in_proj(xt, mix_norm[l][None], w_in_p, l)
        prep_params = (
            fox_q_norm[l][None], fox_k_norm[l][None], _lane_row(fox_fgate_b[l], SM_FG),
            dsa_cq_norm[l][None],
            dsa_q_norm[l][None], dsa_k_norm[l][None], _lane_row(dsa_kidx_norm[l], SM_KI),
        )
        fq, fk, fv, sm, dq, qi, dk, dv, ki = _prep(proj, prep_params, wuq, wuqi, l, tab_a + tab_i, S)

        r3 = lambda a: a.reshape(B, S, a.shape[-1])
        cum = _cumsum(r3(sm))
        c_row = jnp.swapaxes(cum[:, :, SM_FG:SM_FG + FOX_HEADS], 1, 2)
        fox_o = _fox(r3(fq), r3(fk), r3(fv), c_row)

        dtb_l = _lane_row(ssm_dt_bias[l], SM_DT)
        alog_l = _lane_row(ssm_a_log[l], SM_DT)
        ssd_params = (
            ssm_conv_w[l], ssm_conv_b[l][None], dtb_l, alog_l,
            jnp.broadcast_to(dtb_l.T, (LANES, LANES)), jnp.broadcast_to(alog_l.T, (LANES, LANES)),
            jnp.repeat(ssm_d[l], SSM_HEAD_DIM)[None], ssm_norm[l][None],
        )
        ssm_o = _ssd(r3(proj), ssd_params)

        dsa_o = _dsa(r3(dq), r3(qi), r3(sm), r3(dk), r3(dv), r3(ki), topk)

        xt = _out_proj(xt, fox_o.reshape(T, FOX_W), ssm_o.reshape(T, SSM_D_INNER),
                       dsa_o.reshape(T, DSA_W), wo, l)

        xt = _ffn(xt, ffn2_norm[l][None], f2g, f2u, f2d, l)
    return xt.reshape(B, S, D)
```

```python
import functools
import math

import jax
import jax.numpy as jnp
from jax import lax
from jax.experimental import pallas as pl
from jax.experimental.pallas import tpu as pltpu

F32 = jnp.float32
BF16 = jnp.bfloat16
I32 = jnp.int32

D_MODEL = 2048
D_FF = 2 * D_MODEL
NORM_EPS = 1e-6
GATED_NORM_EPS = 1e-5
ROPE_THETA = 500000.0
ROT_FRACTION = 4
HEAD_DIM = 128
FOX_HEADS = 4
SSM_D_INNER = D_MODEL // 2
SSM_HEAD_DIM = 64
SSM_HEADS = SSM_D_INNER // SSM_HEAD_DIM
SSM_GROUPS = 2
SSM_STATE = 128
SSM_CONV = 4
SSM_CHUNK = 128
SSM_CONV_CH = SSM_D_INNER + 2 * SSM_GROUPS * SSM_STATE
DSA_HEADS = 4
DSA_Q_LORA = 512
IDX_HEADS = 8
IDX_HEAD_DIM = 64
TOPK_MAX = 256
FOX_W = FOX_HEADS * HEAD_DIM
DSA_W = DSA_HEADS * HEAD_DIM
IDX_W = IDX_HEADS * 128

LANES = 128
SUBLANES = 8
V7X_VMEM_BUDGET = 56 * 1024 * 1024

P_XBC = 0
P_FQKV = P_XBC + SSM_CONV_CH
P_Z = P_FQKV + 3 * FOX_W
P_TAIL = P_Z + SSM_D_INNER
T_CQ = 0
T_DK = T_CQ + DSA_Q_LORA
T_DV = T_DK + HEAD_DIM
T_SM = T_DV + HEAD_DIM
TAIL_W = 1024
NP = P_TAIL + TAIL_W
SM_KI = 0
SM_FG = IDX_HEAD_DIM
SM_DT = SM_FG + FOX_HEADS
SM_WI = SM_DT + SSM_HEADS

NEG_BIG = -1e30
INT_MIN = -2 ** 31
LOG2E = math.log2(math.e)
QK_SCALE_LOG2 = HEAD_DIM ** -0.5 * LOG2E


def _cparams(sem, vmem=V7X_VMEM_BUDGET):
    return pltpu.CompilerParams(dimension_semantics=sem, vmem_limit_bytes=vmem)


def _sigmoid(x):
    return 1.0 / (1.0 + jnp.exp(-x))


def _softplus(x):
    return jnp.maximum(x, 0.0) + jnp.log(1.0 + jnp.exp(-jnp.abs(x)))


def _rms(x, eps):
    return x * lax.rsqrt(jnp.mean(x * x, axis=-1, keepdims=True) + eps)


def _dot_nt(a, b):
    return lax.dot_general(a, b, (((1,), (1,)), ((), ())), preferred_element_type=F32)


def _dot(a, b):
    return jnp.dot(a, b, preferred_element_type=F32)


def _dot_f32(a, b):
    return jnp.dot(a, b, preferred_element_type=F32, precision=lax.Precision.HIGHEST)


def _ffn_kernel(x_ref, g_ref, wg_ref, wu_ref, wd_ref, o_ref, h_ref, *, nj):
    j = pl.program_id(1)

    @pl.when(j == 0)
    def _():
        h_ref[...] = (_rms(x_ref[...], NORM_EPS) * g_ref[...]).astype(BF16)
        o_ref[...] = jnp.zeros_like(o_ref)

    h = h_ref[...]
    gate = _dot(h, wg_ref[...])
    up = _dot(h, wu_ref[...])
    a = (gate * _sigmoid(gate) * up).astype(BF16)
    o_ref[...] += _dot(a, wd_ref[...])

    @pl.when(j == nj - 1)
    def _():
        o_ref[...] = x_ref[...] + 0.5 * o_ref[...]


def _ffn(x, g, wg, wu, wd, layer, *, tm=1024, tf=512):
    T, D = x.shape
    F = wg.shape[2]
    nj = F // tf
    return pl.pallas_call(
        functools.partial(_ffn_kernel, nj=nj),
        grid=(T // tm, nj),
        in_specs=[
            pl.BlockSpec((tm, D), lambda i, j: (i, 0)),
            pl.BlockSpec((1, D), lambda i, j: (0, 0)),
            pl.BlockSpec((None, D, tf), lambda i, j: (layer, 0, j)),
            pl.BlockSpec((None, D, tf), lambda i, j: (layer, 0, j)),
            pl.BlockSpec((None, tf, D), lambda i, j: (layer, j, 0)),
        ],
        out_specs=pl.BlockSpec((tm, D), lambda i, j: (i, 0)),
        out_shape=jax.ShapeDtypeStruct((T, D), F32),
        scratch_shapes=[pltpu.VMEM((tm, D), BF16)],
        compiler_params=_cparams(("parallel", "arbitrary")),
        name="ffn",
    )(x, g, wg, wu, wd)


def _in_proj_kernel(x_ref, g_ref, w_ref, o_ref, h_ref):
    @pl.when(pl.program_id(1) == 0)
    def _():
        h_ref[...] = (_rms(x_ref[...], NORM_EPS) * g_ref[...]).astype(BF16)

    o_ref[...] = _dot(h_ref[...], w_ref[...])


def _in_proj(x, g, w, layer, *, tm=1024, tn=1024):
    T, D = x.shape
    N = w.shape[2]
    return pl.pallas_call(
        _in_proj_kernel,
        grid=(T // tm, N // tn),
        in_specs=[
            pl.BlockSpec((tm, D), lambda i, j: (i, 0)),
            pl.BlockSpec((1, D), lambda i, j: (0, 0)),
            pl.BlockSpec((None, D, tn), lambda i, j: (layer, 0, j)),
        ],
        out_specs=pl.BlockSpec((tm, tn), lambda i, j: (i, j)),
        out_shape=jax.ShapeDtypeStruct((T, N), F32),
        scratch_shapes=[pltpu.VMEM((tm, D), BF16)],
        compiler_params=_cparams(("parallel", "arbitrary")),
        name="in_proj",
    )(x, g, w)


def _rope(x, c, s_lo, s_hi, half):
    n = x.shape[-1]
    return x * c + pltpu.roll(x, n - half, 1) * s_lo + pltpu.roll(x, half, 1) * s_hi


def _prep_kernel(fqkv_ref, tail_ref, fqg_ref, fkg_ref, fgb_ref, cqg_ref, wuq_ref, wuqi_ref,
                 dqg_ref, dkg_ref, kig_ref, ca_ref, sa1_ref, sa2_ref, ci_ref, si1_ref, si2_ref,
                 fq_ref, fk_ref, fv_ref, sm_ref, dq_ref, qi_ref, dk_ref, dv_ref, ki_ref, *, w_scale):
    for h in range(FOX_HEADS):
        lo, hi = h * HEAD_DIM, (h + 1) * HEAD_DIM
        q = fqkv_ref[:, lo:hi]
        k = fqkv_ref[:, FOX_W + lo:FOX_W + hi]
        fq_ref[:, lo:hi] = (_rms(q, NORM_EPS) * fqg_ref[...] * QK_SCALE_LOG2).astype(BF16)
        fk_ref[:, lo:hi] = (_rms(k, NORM_EPS) * fkg_ref[...]).astype(BF16)
    fv_ref[...] = fqkv_ref[:, 2 * FOX_W:3 * FOX_W].astype(BF16)

    sm = tail_ref[:, T_SM:T_SM + LANES]
    lane = lax.broadcasted_iota(I32, sm.shape, 1)
    zf = sm + fgb_ref[...]
    logf = jnp.minimum(zf, 0.0) - jnp.log(1.0 + jnp.exp(-jnp.abs(zf)))
    is_fg = (lane >= SM_FG) & (lane < SM_FG + FOX_HEADS)
    sm_ref[...] = jnp.where(is_fg, logf, sm * w_scale)

    is_ki = lane < IDX_HEAD_DIM
    ms = jnp.sum(jnp.where(is_ki, sm * sm, 0.0), axis=-1, keepdims=True) * (1.0 / IDX_HEAD_DIM)
    kin = sm * lax.rsqrt(ms + NORM_EPS) * kig_ref[...]
    kin = _rope(kin, ci_ref[...], si1_ref[...], si2_ref[...], IDX_HEAD_DIM // ROT_FRACTION // 2)
    ki_ref[...] = kin.astype(BF16)

    dk = _rms(tail_ref[:, T_DK:T_DK + HEAD_DIM], NORM_EPS) * dkg_ref[...]
    dk_ref[...] = _rope(dk, ca_ref[...], sa1_ref[...], sa2_ref[...], HEAD_DIM // ROT_FRACTION // 2).astype(BF16)
    dv_ref[...] = tail_ref[:, T_DV:T_DV + HEAD_DIM].astype(BF16)

    cq = (_rms(tail_ref[:, T_CQ:T_CQ + DSA_Q_LORA], NORM_EPS) * cqg_ref[...]).astype(BF16)
    uq = _dot(cq, wuq_ref[...])
    uqi = _dot(cq, wuqi_ref[...])
    for h in range(DSA_HEADS):
        lo, hi = h * HEAD_DIM, (h + 1) * HEAD_DIM
        qh = _rms(uq[:, lo:hi], NORM_EPS) * dqg_ref[...]
        qh = _rope(qh, ca_ref[...], sa1_ref[...], sa2_ref[...], HEAD_DIM // ROT_FRACTION // 2)
        dq_ref[:, lo:hi] = (qh * QK_SCALE_LOG2).astype(BF16)
    for p in range(IDX_W // LANES):
        lo, hi = p * LANES, (p + 1) * LANES
        qi_ref[:, lo:hi] = _rope(uqi[:, lo:hi], ci_ref[...], si1_ref[...], si2_ref[...],
                                 IDX_HEAD_DIM // ROT_FRACTION // 2).astype(BF16)


def _prep(proj, params, wuq, wuqi, layer, tables, S, *, tm=256):
    T = proj.shape[0]
    ns = S // tm
    w_scale = (IDX_HEADS ** -0.5) * (IDX_HEAD_DIM ** -0.5)
    row = lambda w: pl.BlockSpec((1, w), lambda i: (0, 0))
    tab = pl.BlockSpec((tm, LANES), lambda i: (i % ns, 0))
    out_bf = lambda w: jax.ShapeDtypeStruct((T, w), BF16)
    out_blk = lambda w: pl.BlockSpec((tm, w), lambda i: (i, 0))
    fqg, fkg, fgb, cqg, dqg, dkg, kig = params
    return pl.pallas_call(
        functools.partial(_prep_kernel, w_scale=w_scale),
        grid=(T // tm,),
        in_specs=[
            pl.BlockSpec((tm, 3 * FOX_W), lambda i: (i, P_FQKV // (3 * FOX_W))),
            pl.BlockSpec((tm, TAIL_W), lambda i: (i, P_TAIL // TAIL_W)),
            row(LANES), row(LANES), row(LANES), row(DSA_Q_LORA),
            pl.BlockSpec((None, DSA_Q_LORA, DSA_W), lambda i: (layer, 0, 0)),
            pl.BlockSpec((None, DSA_Q_LORA, IDX_W), lambda i: (layer, 0, 0)),
            row(LANES), row(LANES), row(LANES),
            tab, tab, tab, tab, tab, tab,
        ],
        out_specs=[out_blk(FOX_W), out_blk(FOX_W), out_blk(FOX_W), out_blk(LANES),
                   out_blk(DSA_W), out_blk(IDX_W), out_blk(HEAD_DIM), out_blk(HEAD_DIM),
                   out_blk(LANES)],
        out_shape=[out_bf(FOX_W), out_bf(FOX_W), out_bf(FOX_W),
                   jax.ShapeDtypeStruct((T, LANES), F32),
                   out_bf(DSA_W), out_bf(IDX_W), out_bf(HEAD_DIM), out_bf(HEAD_DIM),
                   out_bf(LANES)],
        compiler_params=_cparams(("parallel",)),
        name="prep",
    )(proj, proj, fqg, fkg, fgb, cqg, wuq, wuqi, dqg, dkg, kig, *tables)


def _cumsum_kernel(x_ref, o_ref, *, ch):
    S = x_ref.shape[0]
    r = lax.broadcasted_iota(I32, (ch, ch), 0)
    c = lax.broadcasted_iota(I32, (ch, ch), 1)
    tri = jnp.where(c <= r, 1.0, 0.0).astype(F32)

    def body(i, carry):
        off = pl.multiple_of(i * ch, ch)
        cs = _dot_f32(tri, x_ref[pl.ds(off, ch), :]) + carry
        o_ref[pl.ds(off, ch), :] = cs
        return cs[ch - 1:ch, :]

    lax.fori_loop(0, S // ch, body, jnp.zeros((1, x_ref.shape[1]), F32))


def _cumsum(x3, *, ch=128):
    B, S, W = x3.shape
    return pl.pallas_call(
        functools.partial(_cumsum_kernel, ch=ch),
        grid=(B,),
        in_specs=[pl.BlockSpec((None, S, W), lambda b: (b, 0, 0))],
        out_specs=pl.BlockSpec((None, S, W), lambda b: (b, 0, 0)),
        out_shape=jax.ShapeDtypeStruct((B, S, W), F32),
        compiler_params=_cparams(("parallel",)),
        name="cumsum",
    )(x3)


def _softmax_attend(logits, values, n_plain, tail, t, s_ref, mx_ref, l_ref, acc_ref):
    ng = t // LANES

    def pass_a(j, is_tail):
        s = logits(j, is_tail)
        part = None
        for g in range(ng):
            sg = s[:, g * LANES:(g + 1) * LANES]
            s_ref[j * ng + g] = sg
            part = sg if part is None else jnp.maximum(part, sg)
        mx_ref[...] = jnp.maximum(mx_ref[...], part)

    def pass_b(j):
        ps =[jnp.exp2(s_ref[j * ng + g] - mx_ref[...]) for g in range(ng)]
        part = ps[0]
        for g in range(1, ng):
            part = part + ps[g]
        l_ref[...] += part
        acc_ref[...] += _dot(jnp.concatenate(ps, axis=1).astype(BF16), values(j))

    def loop(fn):
        def body(jj, carry):
            fn(2 * jj)
            fn(2 * jj + 1)
            return carry
        lax.fori_loop(0, lax.shift_right_logical(n_plain, 1), body, 0)

        @pl.when((n_plain & 1) == 1)
        def _():
            fn(n_plain - 1)

    mx_ref[...] = jnp.full(mx_ref.shape, -jnp.inf, F32)
    loop(lambda j: pass_a(j, False))
    if tail is not None:
        pass_a(tail, True)
    mx_ref[...] = jnp.broadcast_to(jnp.max(mx_ref[...], axis=-1, keepdims=True), mx_ref.shape)
    l_ref[...] = jnp.zeros_like(l_ref)
    acc_ref[...] = jnp.zeros_like(acc_ref)
    loop(pass_b)
    if tail is not None:
        pass_b(tail)
    return acc_ref[...] / jnp.sum(l_ref[...], axis=-1, keepdims=True)


def _fox_kernel(q_ref, k_ref, v_ref, cr_ref, o_ref, s_ref, mx_ref, l_ref, acc_ref, *, t):
    i = pl.program_id(1)
    row = lax.broadcasted_iota(I32, (t, t), 0)
    col = lax.broadcasted_iota(I32, (t, t), 1)
    for h in range(FOX_HEADS):
        lo, hi = h * HEAD_DIM, (h + 1) * HEAD_DIM

        def logits(j, diagonal, h=h, lo=lo, hi=hi):
            off = pl.multiple_of(j * t, t)
            s = _dot_nt(q_ref[:, lo:hi], k_ref[pl.ds(off, t), lo:hi])
            s = s + cr_ref[h:h + 1, pl.ds(off, t)] * (-LOG2E)
            return jnp.where(col <= row, s, -jnp.inf) if diagonal else s

        def values(j, lo=lo, hi=hi):
            return v_ref[pl.ds(pl.multiple_of(j * t, t), t), lo:hi]

        out = _softmax_attend(logits, values, i, i, t, s_ref, mx_ref, l_ref, acc_ref)
        o_ref[:, lo:hi] = out.astype(o_ref.dtype)


def _fox(q, k, v, c_row, *, t=512):
    B, S, W = q.shape
    return pl.pallas_call(
        functools.partial(_fox_kernel, t=t),
        grid=(B, S // t),
        in_specs=[
            pl.BlockSpec((None, t, W), lambda b, i: (b, i, 0)),
            pl.BlockSpec((None, S, W), lambda b, i: (b, 0, 0)),
            pl.BlockSpec((None, S, W), lambda b, i: (b, 0, 0)),
            pl.BlockSpec((None, FOX_HEADS, S), lambda b, i: (b, 0, 0)),
        ],
        out_specs=pl.BlockSpec((None, t, W), lambda b, i: (b, i, 0)),
        out_shape=jax.ShapeDtypeStruct((B, S, W), BF16),
        scratch_shapes=[
            pltpu.VMEM((S // LANES, t, LANES), F32),
            pltpu.VMEM((t, LANES), F32),
            pltpu.VMEM((t, LANES), F32),
            pltpu.VMEM((t, HEAD_DIM), F32),
        ],
        compiler_params=_cparams(("parallel", "arbitrary")),
        name="fox",
    )(q, k, v, c_row)


def _expand_heads(x, e):
    hi = x.astype(BF16)
    r = x - hi.astype(F32)
    mid = r.astype(BF16)
    lo = (r - mid.astype(F32)).astype(BF16)
    return _dot(hi, e) + _dot(mid, e) + _dot(lo, e)


def _ssd_kernel(xbc_ref, z_ref, sm_ref, cw_ref, cb_ref, dtbl_ref, alogl_ref, dsk_ref, ng_ref, o_ref,
                xpad, state):
    c = pl.program_id(1)
    L = SSM_CHUNK
    PAD = SUBLANES
    P = SSM_HEAD_DIM
    hpg = SSM_HEADS // SSM_GROUPS
    gw = SSM_D_INNER // SSM_GROUPS

    @pl.when(c == 0)
    def _():
        xpad[0:PAD, :] = jnp.zeros((PAD, SSM_CONV_CH), F32)
        state[...] = jnp.zeros_like(state)

    @pl.when(c > 0)
    def _():
        xpad[0:PAD, :] = xpad[L:L + PAD, :]

    xpad[PAD:PAD + L, :] = xbc_ref[...]
    conv = cb_ref[...] + cw_ref[0:1, :] * xpad[PAD - 3:PAD - 3 + L, :]
    for kk in range(1, SSM_CONV):
        conv = conv + cw_ref[kk:kk + 1, :] * xpad[PAD - 3 + kk:PAD - 3 + kk + L, :]
    xc = conv * _sigmoid(conv)
    gn = SSM_GROUPS * SSM_STATE
    xs = xc[:, :SSM_D_INNER]

    dt_c = _softplus(sm_ref[...] + dtbl_ref[...])
    a_c = dt_c * (-jnp.exp(alogl_ref[...]))
    ri = lax.broadcasted_iota(I32, (L, L), 0)
    ci = lax.broadcasted_iota(I32, (L, L), 1)
    tril = ci <= ri
    acs_c = _dot_f32(jnp.where(tril, 1.0, 0.0).astype(F32), a_c)
    acs_r = acs_c.T

    er = lax.broadcasted_iota(I32, (LANES, SSM_D_INNER), 0)
    ec = lax.broadcasted_iota(I32, (LANES, SSM_D_INNER), 1)
    expand = jnp.where(er == SM_DT + lax.shift_right_logical(ec, P.bit_length() - 1), 1.0, 0.0).astype(BF16)
    a_exp = _expand_heads(acs_c, expand)
    dt_exp = _expand_heads(dt_c, expand)
    a_last = a_exp[L - 1:L, :]
    xdt = xs * dt_exp
    xdec_b =(xdt * jnp.exp(a_last - a_exp)).astype(BF16)
    e_col = jnp.exp(a_exp)
    e_last = jnp.exp(a_last)
    lane = lax.broadcasted_iota(I32, (L, LANES), 1)
    first_half = lane < P

    ys = []
    for g in range(SSM_GROUPS):
        bg = xc[:, SSM_D_INNER + g * SSM_STATE:SSM_D_INNER + (g + 1) * SSM_STATE]
        cg_b = xc[:, SSM_D_INNER + gn + g * SSM_STATE:SSM_D_INNER + gn + (g + 1) * SSM_STATE].astype(BF16)
        gmat = _dot_nt(cg_b, bg.astype(BF16))
        bgt_b = bg.T.astype(BF16)
        st = state[:, g * gw:(g + 1) * gw]
        y_off = _dot(cg_b, st.astype(BF16)) * e_col[:, g * gw:(g + 1) * gw]
        state[:, g * gw:(g + 1) * gw] = st * e_last[:, g * gw:(g + 1) * gw] + _dot(bgt_b, xdec_b[:, g * gw:(g + 1) * gw])
        pieces = []
        for pp in range(hpg * P // LANES):
            lo = g * gw + pp * LANES
            x_pair = xdt[:, lo:lo + LANES]
            acc = None
            for half in range(LANES // P):
                h = (lo + half * P) // P
                colv = acs_c[:, SM_DT + h:SM_DT + h + 1]
                rowv = acs_r[SM_DT + h:SM_DT + h + 1, :]
                seg = jnp.exp(jnp.where(tril, colv - rowv, -jnp.inf))
                keep = first_half if half == 0 else jnp.logical_not(first_half)
                d = _dot((gmat * seg).astype(BF16), jnp.where(keep, x_pair, 0.0).astype(BF16))
                acc = d if acc is None else acc + d
            pieces.append(acc)
        ys.append(jnp.concatenate(pieces, axis=1) + y_off)
    y = jnp.concatenate(ys, axis=1) + xs * dsk_ref[...]

    zz = z_ref[...]
    u = y * (zz * _sigmoid(zz))
    for g in range(SSM_GROUPS):
        ug = _rms(u[:, g * gw:(g + 1) * gw], GATED_NORM_EPS)
        o_ref[:, g * gw:(g + 1) * gw] = (ug * ng_ref[:, g * gw:(g + 1) * gw]).astype(o_ref.dtype)


def _ssd(proj3, params):
    B, S, _ = proj3.shape
    L = SSM_CHUNK
    full = lambda a: pl.BlockSpec(a.shape, lambda b, c: (0,) * a.ndim)
    return pl.pallas_call(
        _ssd_kernel,
        grid=(B, S // L),
        in_specs=[
            pl.BlockSpec((None, L, SSM_CONV_CH), lambda b, c: (b, c, P_XBC // SSM_CONV_CH)),
            pl.BlockSpec((None, L, SSM_D_INNER), lambda b, c: (b, c, P_Z // SSM_D_INNER)),
            pl.BlockSpec((None, L, LANES), lambda b, c: (b, c, (P_TAIL + T_SM) // LANES)),
        ] + [full(a) for a in params],
        out_specs=pl.BlockSpec((None, L, SSM_D_INNER), lambda b, c: (b, c, 0)),
        out_shape=jax.ShapeDtypeStruct((B, S, SSM_D_INNER), BF16),
        scratch_shapes=[
            pltpu.VMEM((L + SUBLANES, SSM_CONV_CH), F32),
            pltpu.VMEM((SSM_STATE, SSM_D_INNER), F32),
        ],
        compiler_params=_cparams(("parallel", "arbitrary")),
        name="ssd",
    )(proj3, proj3, proj3, *params)


def _dsa_kernel(q_ref, qi_ref, wi_ref, k_ref, v_ref, ki_ref, o_ref,
                key_ref, bias_ref, s_ref, k16_ref, pref_ref, cge_ref, cand_ref, cnt_ref, tie_ref,
                cand16_ref, cnt16_ref, mx_ref, l_ref, acc_ref, *, t, topk):
    i = pl.program_id(1)
    nk = i + 1
    ng = t // LANES
    imin = jnp.int32(INT_MIN)
    ones = jnp.ones((LANES, LANES), BF16)
    row = lax.broadcasted_iota(I32, (t, t), 0)
    col = lax.broadcasted_iota(I32, (t, t), 1)

    def scores(j, diagonal):
        off = pl.multiple_of(j * t, t)
        kit = ki_ref[pl.ds(off, t), :]
        acc = jnp.zeros((t, t), F32)
        for h in range(IDX_HEADS):
            s = _dot_nt(qi_ref[:, h * LANES:(h + 1) * LANES], kit)
            acc = acc + wi_ref[:, SM_WI + h:SM_WI + h + 1] * jnp.maximum(s, 0.0)
        acc = jnp.where(acc == 0.0, 0.0, acc)
        bits = pltpu.bitcast(acc, I32)
        key = bits ^ ((bits >> 31) & jnp.int32(0x7FFFFFFF))
        if diagonal:
            key = jnp.where(col <= row, key, imin)
        for g in range(ng):
            key_ref[j * ng + g] = key[:, g * LANES:(g + 1) * LANES]

    def scores_body(j, carry):
        scores(j, False)
        return carry

    lax.fori_loop(0, i, scores_body, 0)
    scores(i, True)

    def count(pred):
        cnt_ref[...] = jnp.zeros_like(cnt_ref)

        def body(j, carry):
            part = jnp.where(pred(key_ref[j * ng]), 1.0, 0.0)
            for g in range(1, ng):
                part = part + jnp.where(pred(key_ref[j * ng + g]), 1.0, 0.0)
            cnt_ref[...] += part
            return carry

        lax.fori_loop(0, nk, body, 0)
        return _dot(cnt_ref[...].astype(BF16), ones)

    kf = float(topk)
    half = 1 << 15
    one16, zero16 = jnp.ones((), BF16), jnp.zeros((), BF16)

    def count16(pred):
        cnt16_ref[...] = jnp.zeros_like(cnt16_ref)

        def body(j, carry):
            part = jnp.where(pred(k16_ref[j * ng]), one16, zero16)
            for g in range(1, ng):
                part = part + jnp.where(pred(k16_ref[j * ng + g]), one16, zero16)
            cnt16_ref[...] += part
            return carry

        lax.fori_loop(0, nk, body, 0)
        return _dot(cnt16_ref[...], ones)

    def search16(kneed):
        pref_ref[...] = jnp.zeros_like(pref_ref)
        cge_ref[...] = jnp.zeros_like(cge_ref)

        def step(b, carry):
            cand_u = pref_ref[...] | lax.shift_left(jnp.int32(1), 15 - b)
            cand16_ref[...] = (cand_u - half).astype(jnp.int16)
            cnt = count16(lambda kt: kt >= cand16_ref[...])
            ok = cnt >= kneed
            pref_ref[...] = jnp.where(ok, cand_u, pref_ref[...])
            cge_ref[...] = jnp.where(ok, cnt, cge_ref[...])
            return carry

        lax.fori_loop(0, 16, step, 0)
        return pref_ref[...], cge_ref[...]

    def fill_hi(jj, carry):
        k16_ref[jj] = (key_ref[jj] >> 16).astype(jnp.int16)
        return carry

    lax.fori_loop(0, nk * ng, fill_hi, 0)
    hi_u, cge_hi = search16(kf)
    hi_s = hi_u - half
    cand16_ref[...] = hi_s.astype(jnp.int16)
    need_lo = kf - count16(lambda kt: kt > cand16_ref[...])

    cand_ref[...] = hi_s

    def fill_lo(jj, carry):
        key = key_ref[jj]
        lo_s = (key & jnp.int32(0xFFFF)) - half
        k16_ref[jj] = jnp.where((key >> 16) == cand_ref[...], lo_s, -half).astype(jnp.int16)
        return carry

    lax.fori_loop(0, nk * ng, fill_lo, 0)
    lo_u, cge_lo = search16(need_lo)
    cand_ref[...] = hi_s * (1 << 16) + lo_u

    c_ge = jnp.where(lo_u != 0, (kf - need_lo) + cge_lo, cge_hi)
    tie_row = ((hi_u != 0) | (lo_u != 0)) & (c_ge > kf)
    tie_ref[...] = jnp.where(tie_row, 1.0, 0.0)
    n_tie = jnp.sum(tie_ref[...])

    @pl.when(n_tie > 0.0)
    def _():
        need = kf - count(lambda kt: kt > cand_ref[...])
        rr = lax.broadcasted_iota(I32, (LANES, LANES), 0)
        cc = lax.broadcasted_iota(I32, (LANES, LANES), 1)
        upper = jnp.where(rr < cc, 1.0, 0.0).astype(BF16)
        tied = tie_ref[...] > 0.0
        cnt_ref[...] = jnp.zeros_like(cnt_ref)

        def fix(jj, carry):
            kt = key_ref[jj]
            eq = (kt == cand_ref[...]) & tied
            eqb = jnp.where(eq, 1.0, 0.0).astype(BF16)
            rank = cnt_ref[...] + _dot(eqb, upper)
            drop = eq & (rank >= need)
            key_ref[jj] = jnp.where(drop, imin, kt)
            cnt_ref[...] += _dot(eqb, ones)
            return carry

        lax.fori_loop(0, nk * ng, fix, 0)

    cand_ref[...] = jnp.maximum(cand_ref[...], imin + 1)

    def fill_bias(jj, carry):
        bias_ref[jj] = jnp.where(key_ref[jj] >= cand_ref[...], 0.0, NEG_BIG)
        return carry

    lax.fori_loop(0, nk * ng, fill_bias, 0)

    for h in range(DSA_HEADS):
        lo, hi = h * HEAD_DIM, (h + 1) * HEAD_DIM

        def logits(j, is_tail, lo=lo, hi=hi):
            off = pl.multiple_of(j * t, t)
            bias = jnp.concatenate([bias_ref[j * ng + g] for g in range(ng)], axis=1)
            return _dot_nt(q_ref[:, lo:hi], k_ref[pl.ds(off, t), :]) + bias

        def values(j):
            return v_ref[pl.ds(pl.multiple_of(j * t, t), t), :]

        out = _softmax_attend(logits, values, nk, None, t, s_ref, mx_ref, l_ref, acc_ref)
        o_ref[:, lo:hi] = out.astype(o_ref.dtype)


def _dsa(dq, qi, wi, dk, dv, ki, topk, *, t=512):
    B, S, W = dq.shape
    rep = lambda dt: pltpu.VMEM((t, LANES), dt)
    return pl.pallas_call(
        functools.partial(_dsa_kernel, t=t, topk=topk),
        grid=(B, S // t),
        in_specs=[
            pl.BlockSpec((None, t, W), lambda b, i: (b, i, 0)),
            pl.BlockSpec((None, t, IDX_W), lambda b, i: (b, i, 0)),
            pl.BlockSpec((None, t, LANES), lambda b, i: (b, i, 0)),
            pl.BlockSpec((None, S, HEAD_DIM), lambda b, i: (b, 0, 0)),
            pl.BlockSpec((None, S, HEAD_DIM), lambda b, i: (b, 0, 0)),
            pl.BlockSpec((None, S, LANES), lambda b, i: (b, 0, 0)),
        ],
        out_specs=pl.BlockSpec((None, t, W), lambda b, i: (b, i, 0)),
        out_shape=jax.ShapeDtypeStruct((B, S, W), BF16),
        scratch_shapes=[
            pltpu.VMEM((S // LANES, t, LANES), I32),
            pltpu.VMEM((S // LANES, t, LANES), F32),
            pltpu.VMEM((S // LANES, t, LANES), F32),
            pltpu.VMEM((S // LANES, t, LANES), jnp.int16),
            rep(I32), rep(F32), rep(I32), rep(F32), rep(F32),
            rep(jnp.int16), rep(BF16),
            rep(F32), rep(F32),
            pltpu.VMEM((t, HEAD_DIM), F32),
        ],
        compiler_params=_cparams(("parallel", "arbitrary")),
        name="dsa",
    )(dq, qi, wi, dk, dv, ki)


def _out_proj_kernel(x_ref, a_ref, s_ref, d_ref, w_ref, o_ref):
    acc = _dot(a_ref[...], w_ref[0:FOX_W, :])
    acc = acc + _dot(s_ref[...], w_ref[FOX_W:FOX_W + SSM_D_INNER, :])
    acc = acc + _dot(d_ref[...], w_ref[FOX_W + SSM_D_INNER:, :])
    o_ref[...] = x_ref[...] + acc


def _out_proj(x, fox_o, ssm_o, dsa_o, w, layer, *, tm=512):
    T, D = x.shape
    blk = lambda wd: pl.BlockSpec((tm, wd), lambda i: (i, 0))
    return pl.pallas_call(
        _out_proj_kernel,
        grid=(T // tm,),
        in_specs=[blk(D), blk(FOX_W), blk(SSM_D_INNER), blk(DSA_W),
                  pl.BlockSpec((None,) + w.shape[1:], lambda i: (layer, 0, 0))],
        out_specs=blk(D),
        out_shape=jax.ShapeDtypeStruct((T, D), F32),
        compiler_params=_cparams(("parallel",)),
        name="out_proj",
    )(x, fox_o, ssm_o, dsa_o, w)


def _relayout_w_in(w_in):
    sizes = (FOX_W, FOX_W, FOX_W, FOX_HEADS, SSM_D_INNER, SSM_CONV_CH, SSM_HEADS,
             DSA_Q_LORA, HEAD_DIM, HEAD_DIM, IDX_HEAD_DIM, IDX_HEADS)
    offs = [0]
    for s in sizes:
        offs.append(offs[-1] + s)
    seg = lambda n: w_in[:, :, offs[n]:offs[n + 1]].astype(BF16)
    fq, fk, fv, fg, z, xbc, dt, cq, dk, dv, dki, dwi = (seg(n) for n in range(len(sizes)))
    zeros = lambda n: jnp.zeros(w_in.shape[:2] + (n,), BF16)
    small = jnp.concatenate([dki, fg, dt, dwi, zeros(LANES - SM_WI - IDX_HEADS)], axis=-1)
    tail = jnp.concatenate([cq, dk, dv, small, zeros(TAIL_W - T_SM - LANES)], axis=-1)
    return jnp.concatenate([xbc, fq, fk, fv, z, tail], axis=-1)


def _rope_tables(S, head_dim):
    rot = head_dim // ROT_FRACTION
    half = rot // 2
    inv = ROPE_THETA ** (-jnp.arange(0, rot, 2, dtype=F32) / rot)
    ang = jnp.arange(S, dtype=F32)[:, None] * inv[None, :]
    cos, sin = jnp.cos(ang), jnp.sin(ang)
    ones = jnp.ones((S, head_dim - rot), F32)
    zer = jnp.zeros((S, head_dim - half), F32)
    c = jnp.concatenate([cos, cos, ones], axis=-1)
    s_lo = jnp.concatenate([-sin, zer], axis=-1)
    s_hi = jnp.concatenate([jnp.zeros((S, half), F32), sin, jnp.zeros((S, head_dim - rot), F32)], axis=-1)
    rep = LANES // head_dim
    return tuple(jnp.tile(t, (1, rep)) for t in (c, s_lo, s_hi))


def _lane_row(v, start):
    return jnp.zeros((1, LANES), F32).at[0, start:start + v.shape[0]].set(v.astype(F32))


def kernel(x, ffn1_norm, ffn1_w_gate, ffn1_w_up, ffn1_w_down, mix_norm, w_in, fox_fgate_b, fox_q_norm, fox_k_norm, ssm_conv_w, ssm_conv_b, ssm_dt_bias, ssm_a_log, ssm_d, ssm_norm, dsa_cq_norm, dsa_w_uq, dsa_w_uq_idx, dsa_q_norm, dsa_k_norm, dsa_kidx_norm, w_out, ffn2_norm, ffn2_w_gate, ffn2_w_up, ffn2_w_down):
    B, S, D = x.shape
    assert D == D_MODEL and S % 512 == 0 and (B * S) % 1024 == 0
    depth = w_in.shape[0]
    T = B * S
    topk = min(TOPK_MAX, S // 4)

    w_in_p = _relayout_w_in(w_in)
    bf = lambda a: a.astype(BF16)
    f1g, f1u, f1d = bf(ffn1_w_gate), bf(ffn1_w_up), bf(ffn1_w_down)
    f2g, f2u, f2d = bf(ffn2_w_gate), bf(ffn2_w_up), bf(ffn2_w_down)
    wuq, wo = bf(dsa_w_uq), bf(w_out)
    wuqi = bf(dsa_w_uq_idx).reshape(depth, DSA_Q_LORA, IDX_HEADS, IDX_HEAD_DIM)
    wuqi = jnp.pad(wuqi, ((0, 0), (0, 0), (0, 0), (0, LANES - IDX_HEAD_DIM))).reshape(depth, DSA_Q_LORA, IDX_W)
    tab_a = _rope_tables(S, HEAD_DIM)
    tab_i = _rope_tables(S, IDX_HEAD_DIM)

    xt = x.reshape(T, D)
    for l in range(depth):
        xt = _ffn(xt, ffn1_norm[l][None], f1g, f1u, f1d, l)

        proj = _in_proj(xt, mix_norm[l][None], w_in_p, l)
        prep_params = (
            fox_q_norm[l][None], fox_k_norm[l][None], _lane_row(fox_fgate_b[l], SM_FG),
            dsa_cq_norm[l][None],
            dsa_q_norm[l][None], dsa_k_norm[l][None], _lane_row(dsa_kidx_norm[l], SM_KI),
        )
        fq, fk, fv, sm, dq, qi, dk, dv, ki = _prep(proj, prep_params, wuq, wuqi, l, tab_a + tab_i, S)

        r3 = lambda a: a.reshape(B, S, a.shape[-1])
        cum = _cumsum(r3(sm))
        c_row = jnp.swapaxes(cum[:, :, SM_FG:SM_FG + FOX_HEADS], 1, 2)
        fox_o = _fox(r3(fq), r3(fk), r3(fv), c_row)

        dtb_l = _lane_row(ssm_dt_bias[l], SM_DT)
        alog_l = _lane_row(ssm_a_log[l], SM_DT)
        ssd_params = (
            ssm_conv_w[l], ssm_conv_b[l][None], dtb_l, alog_l,
            jnp.repeat(ssm_d[l], SSM_HEAD_DIM)[None], ssm_norm[l][None],
        )
        ssm_o = _ssd(r3(proj), ssd_params)

        dsa_o = _dsa(r3(dq), r3(qi), r3(sm), r3(dk), r3(dv), r3(ki), topk)

        xt = _out_proj(xt, fox_o.reshape(T, FOX_W), ssm_o.reshape(T, SSM_D_INNER),
                       dsa_o.reshape(T, DSA_W), wo, l)

        xt = _ffn(xt, ffn2_norm[l][None], f2g, f2u, f2d, l)
    return xt.reshape(B, S, D)
```

```python
import functools
import math

import jax
import jax.numpy as jnp
from jax import lax
from jax.experimental import pallas as pl
from jax.experimental.pallas import tpu as pltpu

F32 = jnp.float32
BF16 = jnp.bfloat16
I32 = jnp.int32

D_MODEL = 2048
D_FF = 2 * D_MODEL
NORM_EPS = 1e-6
GATED_NORM_EPS = 1e-5
ROPE_THETA = 500000.0
ROT_FRACTION = 4
HEAD_DIM = 128
FOX_HEADS = 4
SSM_D_INNER = D_MODEL // 2
SSM_HEAD_DIM = 64
SSM_HEADS = SSM_D_INNER // SSM_HEAD_DIM
SSM_GROUPS = 2
SSM_STATE = 128
SSM_CONV = 4
SSM_CHUNK = 128
SSM_CONV_CH = SSM_D_INNER + 2 * SSM_GROUPS * SSM_STATE
DSA_HEADS = 4
DSA_Q_LORA = 512
IDX_HEADS = 8
IDX_HEAD_DIM = 64
TOPK_MAX = 256
FOX_W = FOX_HEADS * HEAD_DIM
DSA_W = DSA_HEADS * HEAD_DIM
IDX_W = IDX_HEADS * 128

LANES = 128
SUBLANES = 8
V7X_VMEM_BUDGET = 56 * 1024 * 1024

P_XBC = 0
P_FQKV = P_XBC + SSM_CONV_CH
P_Z = P_FQKV + 3 * FOX_W
P_TAIL = P_Z + SSM_D_INNER
T_CQ = 0
T_DK = T_CQ + DSA_Q_LORA
T_DV = T_DK + HEAD_DIM
T_SM = T_DV + HEAD_DIM
TAIL_W = 1024
NP = P_TAIL + TAIL_W
SM_KI = 0
SM_FG = IDX_HEAD_DIM
SM_DT = SM_FG + FOX_HEADS
SM_WI = SM_DT + SSM_HEADS

FOX_GROUP = 4
DSA_GROUP = 2
NEG_BIG = -1e30
INT_MIN = -2 ** 31
LOG2E = math.log2(math.e)
QK_SCALE_LOG2 = HEAD_DIM ** -0.5 * LOG2E


def _cparams(sem, vmem=V7X_VMEM_BUDGET):
    return pltpu.CompilerParams(dimension_semantics=sem, vmem_limit_bytes=vmem)


def _sigmoid(x):
    return 1.0 / (1.0 + jnp.exp(-x))


def _softplus(x):
    return jnp.maximum(x, 0.0) + jnp.log(1.0 + jnp.exp(-jnp.abs(x)))


def _rms(x, eps):
    return x * lax.rsqrt(jnp.mean(x * x, axis=-1, keepdims=True) + eps)


def _dot_nt(a, b):
    return lax.dot_general(a, b, (((1,), (1,)), ((), ())), preferred_element_type=F32)


def _dot(a, b):
    return jnp.dot(a, b, preferred_element_type=F32)


def _dot_f32(a, b):
    return jnp.dot(a, b, preferred_element_type=F32, precision=lax.Precision.HIGHEST)


def _ffn_kernel(x_ref, g_ref, wg_ref, wu_ref, wd_ref, o_ref, h_ref, *, nj):
    j = pl.program_id(1)

    @pl.when(j == 0)
    def _():
        h_ref[...] = (_rms(x_ref[...], NORM_EPS) * g_ref[...]).astype(BF16)
        o_ref[...] = jnp.zeros_like(o_ref)

    h = h_ref[...]
    gate = _dot(h, wg_ref[...])
    up = _dot(h, wu_ref[...])
    a = (gate * _sigmoid(gate) * up).astype(BF16)
    o_ref[...] += _dot(a, wd_ref[...])

    @pl.when(j == nj - 1)
    def _():
        o_ref[...] = x_ref[...] + 0.5 * o_ref[...]


def _ffn(x, g, wg, wu, wd, layer, *, tm=1024, tf=512):
    T, D = x.shape
    F = wg.shape[2]
    nj = F // tf
    return pl.pallas_call(
        functools.partial(_ffn_kernel, nj=nj),
        grid=(T // tm, nj),
        in_specs=[
            pl.BlockSpec((tm, D), lambda i, j: (i, 0)),
            pl.BlockSpec((1, D), lambda i, j: (0, 0)),
            pl.BlockSpec((None, D, tf), lambda i, j: (layer, 0, j)),
            pl.BlockSpec((None, D, tf), lambda i, j: (layer, 0, j)),
            pl.BlockSpec((None, tf, D), lambda i, j: (layer, j, 0)),
        ],
        out_specs=pl.BlockSpec((tm, D), lambda i, j: (i, 0)),
        out_shape=jax.ShapeDtypeStruct((T, D), F32),
        scratch_shapes=[pltpu.VMEM((tm, D), BF16)],
        compiler_params=_cparams(("parallel", "arbitrary")),
        name="ffn",
    )(x, g, wg, wu, wd)


def _in_proj_kernel(x_ref, g_ref, w_ref, o_ref, h_ref):
    @pl.when(pl.program_id(1) == 0)
    def _():
        h_ref[...] = (_rms(x_ref[...], NORM_EPS) * g_ref[...]).astype(BF16)

    o_ref[...] = _dot(h_ref[...], w_ref[...])


def _in_proj(x, g, w, layer, *, tm=1024, tn=1024):
    T, D = x.shape
    N = w.shape[2]
    return pl.pallas_call(
        _in_proj_kernel,
        grid=(T // tm, N // tn),
        in_specs=[
            pl.BlockSpec((tm, D), lambda i, j: (i, 0)),
            pl.BlockSpec((1, D), lambda i, j: (0, 0)),
            pl.BlockSpec((None, D, tn), lambda i, j: (layer, 0, j)),
        ],
        out_specs=pl.BlockSpec((tm, tn), lambda i, j: (i, j)),
        out_shape=jax.ShapeDtypeStruct((T, N), F32),
        scratch_shapes=[pltpu.VMEM((tm, D), BF16)],
        compiler_params=_cparams(("parallel", "arbitrary")),
        name="in_proj",
    )(x, g, w)


def _rope(x, c, s_lo, s_hi, half):
    n = x.shape[-1]
    return x * c + pltpu.roll(x, n - half, 1) * s_lo + pltpu.roll(x, half, 1) * s_hi


def _prep_kernel(fqkv_ref, tail_ref, fqg_ref, fkg_ref, fgb_ref, cqg_ref, wuq_ref, wuqi_ref,
                 dqg_ref, dkg_ref, kig_ref, ca_ref, sa1_ref, sa2_ref, ci_ref, si1_ref, si2_ref,
                 fq_ref, fk_ref, fv_ref, sm_ref, dq_ref, qi_ref, dk_ref, dv_ref, ki_ref, *, w_scale):
    for h in range(FOX_HEADS):
        lo, hi = h * HEAD_DIM, (h + 1) * HEAD_DIM
        q = fqkv_ref[:, lo:hi]
        k = fqkv_ref[:, FOX_W + lo:FOX_W + hi]
        fq_ref[:, lo:hi] = (_rms(q, NORM_EPS) * fqg_ref[...] * QK_SCALE_LOG2).astype(BF16)
        fk_ref[:, lo:hi] = (_rms(k, NORM_EPS) * fkg_ref[...]).astype(BF16)
    fv_ref[...] = fqkv_ref[:, 2 * FOX_W:3 * FOX_W].astype(BF16)

    sm = tail_ref[:, T_SM:T_SM + LANES]
    lane = lax.broadcasted_iota(I32, sm.shape, 1)
    zf = sm + fgb_ref[...]
    logf = jnp.minimum(zf, 0.0) - jnp.log(1.0 + jnp.exp(-jnp.abs(zf)))
    is_fg = (lane >= SM_FG) & (lane < SM_FG + FOX_HEADS)
    sm_ref[...] = jnp.where(is_fg, logf, sm * w_scale)

    is_ki = lane < IDX_HEAD_DIM
    ms = jnp.sum(jnp.where(is_ki, sm * sm, 0.0), axis=-1, keepdims=True) * (1.0 / IDX_HEAD_DIM)
    kin = sm * lax.rsqrt(ms + NORM_EPS) * kig_ref[...]
    kin = _rope(kin, ci_ref[...], si1_ref[...], si2_ref[...], IDX_HEAD_DIM // ROT_FRACTION // 2)
    ki_ref[...] = kin.astype(BF16)

    dk = _rms(tail_ref[:, T_DK:T_DK + HEAD_DIM], NORM_EPS) * dkg_ref[...]
    dk_ref[...] = _rope(dk, ca_ref[...], sa1_ref[...], sa2_ref[...], HEAD_DIM // ROT_FRACTION // 2).astype(BF16)
    dv_ref[...] = tail_ref[:, T_DV:T_DV + HEAD_DIM].astype(BF16)

    cq = (_rms(tail_ref[:, T_CQ:T_CQ + DSA_Q_LORA], NORM_EPS) * cqg_ref[...]).astype(BF16)
    uq = _dot(cq, wuq_ref[...])
    uqi = _dot(cq, wuqi_ref[...])
    for h in range(DSA_HEADS):
        lo, hi = h * HEAD_DIM, (h + 1) * HEAD_DIM
        qh = _rms(uq[:, lo:hi], NORM_EPS) * dqg_ref[...]
        qh = _rope(qh, ca_ref[...], sa1_ref[...], sa2_ref[...], HEAD_DIM // ROT_FRACTION // 2)
        dq_ref[:, lo:hi] = (qh * QK_SCALE_LOG2).astype(BF16)
    for p in range(IDX_W // LANES):
        lo, hi = p * LANES, (p + 1) * LANES
        qi_ref[:, lo:hi] = _rope(uqi[:, lo:hi], ci_ref[...], si1_ref[...], si2_ref[...],
                                 IDX_HEAD_DIM // ROT_FRACTION // 2).astype(BF16)


def _prep(proj, params, wuq, wuqi, layer, tables, S, *, tm=256):
    T = proj.shape[0]
    ns = S // tm
    w_scale = (IDX_HEADS ** -0.5) * (IDX_HEAD_DIM ** -0.5)
    row = lambda w: pl.BlockSpec((1, w), lambda i: (0, 0))
    tab = pl.BlockSpec((tm, LANES), lambda i: (i % ns, 0))
    out_bf = lambda w: jax.ShapeDtypeStruct((T, w), BF16)
    out_blk = lambda w: pl.BlockSpec((tm, w), lambda i: (i, 0))
    fqg, fkg, fgb, cqg, dqg, dkg, kig = params
    return pl.pallas_call(
        functools.partial(_prep_kernel, w_scale=w_scale),
        grid=(T // tm,),
        in_specs=[
            pl.BlockSpec((tm, 3 * FOX_W), lambda i: (i, P_FQKV // (3 * FOX_W))),
            pl.BlockSpec((tm, TAIL_W), lambda i: (i, P_TAIL // TAIL_W)),
            row(LANES), row(LANES), row(LANES), row(DSA_Q_LORA),
            pl.BlockSpec((None, DSA_Q_LORA, DSA_W), lambda i: (layer, 0, 0)),
            pl.BlockSpec((None, DSA_Q_LORA, IDX_W), lambda i: (layer, 0, 0)),
            row(LANES), row(LANES), row(LANES),
            tab, tab, tab, tab, tab, tab,
        ],
        out_specs=[out_blk(FOX_W), out_blk(FOX_W), out_blk(FOX_W), out_blk(LANES),
                   out_blk(DSA_W), out_blk(IDX_W), out_blk(HEAD_DIM), out_blk(HEAD_DIM),
                   out_blk(LANES)],
        out_shape=[out_bf(FOX_W), out_bf(FOX_W), out_bf(FOX_W),
                   jax.ShapeDtypeStruct((T, LANES), F32),
                   out_bf(DSA_W), out_bf(IDX_W), out_bf(HEAD_DIM), out_bf(HEAD_DIM),
                   out_bf(LANES)],
        compiler_params=_cparams(("parallel",)),
        name="prep",
    )(proj, proj, fqg, fkg, fgb, cqg, wuq, wuqi, dqg, dkg, kig, *tables)


def _cumsum_kernel(x_ref, o_ref, *, ch):
    S = x_ref.shape[0]
    r = lax.broadcasted_iota(I32, (ch, ch), 0)
    c = lax.broadcasted_iota(I32, (ch, ch), 1)
    tri = jnp.where(c <= r, 1.0, 0.0).astype(F32)

    def body(i, carry):
        off = pl.multiple_of(i * ch, ch)
        cs = _dot_f32(tri, x_ref[pl.ds(off, ch), :]) + carry
        o_ref[pl.ds(off, ch), :] = cs
        return cs[ch - 1:ch, :]

    lax.fori_loop(0, S // ch, body, jnp.zeros((1, x_ref.shape[1]), F32))


def _cumsum(x3, *, ch=128):
    B, S, W = x3.shape
    return pl.pallas_call(
        functools.partial(_cumsum_kernel, ch=ch),
        grid=(B,),
        in_specs=[pl.BlockSpec((None, S, W), lambda b: (b, 0, 0))],
        out_specs=pl.BlockSpec((None, S, W), lambda b: (b, 0, 0)),
        out_shape=jax.ShapeDtypeStruct((B, S, W), F32),
        compiler_params=_cparams(("parallel",)),
        name="cumsum",
    )(x3)


def _softmax_attend(heads, n_plain, tail, t, s_ref, mx_ref, l_ref, acc_ref):
    ng = t // LANES
    nh = len(heads)

    def pass_a(j, is_tail):
        for n, (logits, _) in enumerate(heads):
            s = logits(j, is_tail)
            part = None
            for g in range(ng):
                sg = s[:, g * LANES:(g + 1) * LANES]
                s_ref[n, j * ng + g] = sg
                part = sg if part is None else jnp.maximum(part, sg)
            mx_ref[n] = jnp.maximum(mx_ref[n], part)

    def pass_b(j):
        for n, (_, values) in enumerate(heads):
            ps = [jnp.exp2(s_ref[n, j * ng + g] - mx_ref[n]) for g in range(ng)]
            part = ps[0]
            for g in range(1, ng):
                part = part + ps[g]
            l_ref[n] += part
            acc_ref[n] += _dot(jnp.concatenate(ps, axis=1).astype(BF16), values(j))

    def loop(fn):
        def body(jj, carry):
            fn(2 * jj)
            fn(2 * jj + 1)
            return carry
        lax.fori_loop(0, lax.shift_right_logical(n_plain, 1), body, 0)

        @pl.when((n_plain & 1) == 1)
        def _():
            fn(n_plain - 1)

    mx_ref[...] = jnp.full(mx_ref.shape, -jnp.inf, F32)
    loop(lambda j: pass_a(j, False))
    if tail is not None:
        pass_a(tail, True)
    for n in range(nh):
        mx_ref[n] = jnp.broadcast_to(jnp.max(mx_ref[n], axis=-1, keepdims=True), mx_ref.shape[1:])
    l_ref[...] = jnp.zeros_like(l_ref)
    acc_ref[...] = jnp.zeros_like(acc_ref)
    loop(pass_b)
    if tail is not None:
        pass_b(tail)
    return [acc_ref[n] / jnp.sum(l_ref[n], axis=-1, keepdims=True) for n in range(nh)]


def _attend_scratch(group, S, t):
    return [
        pltpu.VMEM((group, S // LANES, t, LANES), F32),
        pltpu.VMEM((group, t, LANES), F32),
        pltpu.VMEM((group, t, LANES), F32),
        pltpu.VMEM((group, t, HEAD_DIM), F32),
    ]


def _fox_kernel(q_ref, k_ref, v_ref, cr_ref, o_ref, s_ref, mx_ref, l_ref, acc_ref, *, t):
    i = pl.program_id(1)
    row = lax.broadcasted_iota(I32, (t, t), 0)
    col = lax.broadcasted_iota(I32, (t, t), 1)
    def head(h):
        lo, hi = h * HEAD_DIM, (h + 1) * HEAD_DIM

        def logits(j, diagonal):
            off = pl.multiple_of(j * t, t)
            s = _dot_nt(q_ref[:, lo:hi], k_ref[pl.ds(off, t), lo:hi])
            s = s + cr_ref[h:h + 1, pl.ds(off, t)] * (-LOG2E)
            return jnp.where(col <= row, s, -jnp.inf) if diagonal else s

        def values(j):
            return v_ref[pl.ds(pl.multiple_of(j * t, t), t), lo:hi]

        return logits, values

    for h0 in range(0, FOX_HEADS, FOX_GROUP):
        outs = _softmax_attend([head(h0 + n) for n in range(FOX_GROUP)], i, i, t,
                               s_ref, mx_ref, l_ref, acc_ref)
        for n, out in enumerate(outs):
            o_ref[:, (h0 + n) * HEAD_DIM:(h0 + n + 1) * HEAD_DIM] = out.astype(o_ref.dtype)


def _fox(q, k, v, c_row, *, t=512):
    B, S, W = q.shape
    return pl.pallas_call(
        functools.partial(_fox_kernel, t=t),
        grid=(B, S // t),
        in_specs=[
            pl.BlockSpec((None, t, W), lambda b, i: (b, i, 0)),
            pl.BlockSpec((None, S, W), lambda b, i: (b, 0, 0)),
            pl.BlockSpec((None, S, W), lambda b, i: (b, 0, 0)),
            pl.BlockSpec((None, FOX_HEADS, S), lambda b, i: (b, 0, 0)),
        ],
        out_specs=pl.BlockSpec((None, t, W), lambda b, i: (b, i, 0)),
        out_shape=jax.ShapeDtypeStruct((B, S, W), BF16),
        scratch_shapes=_attend_scratch(FOX_GROUP, S, t),
        compiler_params=_cparams(("parallel", "arbitrary")),
        name="fox",
    )(q, k, v, c_row)


def _expand_heads(x, e):
    hi = x.astype(BF16)
    r = x - hi.astype(F32)
    mid = r.astype(BF16)
    lo = (r - mid.astype(F32)).astype(BF16)
    return _dot(hi, e) + _dot(mid, e) + _dot(lo, e)


def _ssd_kernel(xbc_ref, z_ref, sm_ref, cw_ref, cb_ref, dtbl_ref, alogl_ref, dsk_ref, ng_ref, o_ref,
                xpad, state):
    c = pl.program_id(1)
    L = SSM_CHUNK
    PAD = SUBLANES
    P = SSM_HEAD_DIM
    hpg = SSM_HEADS // SSM_GROUPS
    gw = SSM_D_INNER // SSM_GROUPS

    @pl.when(c == 0)
    def _():
        xpad[0:PAD, :] = jnp.zeros((PAD, SSM_CONV_CH), F32)
        state[...] = jnp.zeros_like(state)

    @pl.when(c > 0)
    def _():
        xpad[0:PAD, :] = xpad[L:L + PAD, :]

    xpad[PAD:PAD + L, :] = xbc_ref[...]
    conv = cb_ref[...] + cw_ref[0:1, :] * xpad[PAD - 3:PAD - 3 + L, :]
    for kk in range(1, SSM_CONV):
        conv = conv + cw_ref[kk:kk + 1, :] * xpad[PAD - 3 + kk:PAD - 3 + kk + L, :]
    xc = conv * _sigmoid(conv)
    gn = SSM_GROUPS * SSM_STATE
    xs = xc[:, :SSM_D_INNER]

    dt_c = _softplus(sm_ref[...] + dtbl_ref[...])
    a_c = dt_c * (-jnp.exp(alogl_ref[...]))
    ri = lax.broadcasted_iota(I32, (L, L), 0)
    ci = lax.broadcasted_iota(I32, (L, L), 1)
    tril = ci <= ri
    acs_c = _dot_f32(jnp.where(tril, 1.0, 0.0).astype(F32), a_c)
    acs_r = acs_c.T

    er = lax.broadcasted_iota(I32, (LANES, SSM_D_INNER), 0)
    ec = lax.broadcasted_iota(I32, (LANES, SSM_D_INNER), 1)
    expand = jnp.where(er == SM_DT + lax.shift_right_logical(ec, P.bit_length() - 1), 1.0, 0.0).astype(BF16)
    a_exp = _expand_heads(acs_c, expand)
    dt_exp = _expand_heads(dt_c, expand)
    a_last = a_exp[L - 1:L, :]
    xdt = xs * dt_exp
    xdec_b =(xdt * jnp.exp(a_last - a_exp)).astype(BF16)
    e_col = jnp.exp(a_exp)
    e_last = jnp.exp(a_last)
    lane = lax.broadcasted_iota(I32, (L, LANES), 1)
    first_half = lane < P

    ys = []
    for g in range(SSM_GROUPS):
        bg = xc[:, SSM_D_INNER + g * SSM_STATE:SSM_D_INNER + (g + 1) * SSM_STATE]
        cg_b = xc[:, SSM_D_INNER + gn + g * SSM_STATE:SSM_D_INNER + gn + (g + 1) * SSM_STATE].astype(BF16)
        gmat = _dot_nt(cg_b, bg.astype(BF16))
        bgt_b = bg.T.astype(BF16)
        st = state[:, g * gw:(g + 1) * gw]
        y_off = _dot(cg_b, st.astype(BF16)) * e_col[:, g * gw:(g + 1) * gw]
        state[:, g * gw:(g + 1) * gw] = st * e_last[:, g * gw:(g + 1) * gw] + _dot(bgt_b, xdec_b[:, g * gw:(g + 1) * gw])
        pieces = []
        for pp in range(hpg * P // LANES):
            lo = g * gw + pp * LANES
            x_pair = xdt[:, lo:lo + LANES]
            acc = None
            for half in range(LANES // P):
                h = (lo + half * P) // P
                colv = acs_c[:, SM_DT + h:SM_DT + h + 1]
                rowv = acs_r[SM_DT + h:SM_DT + h + 1, :]
                seg = jnp.exp(jnp.where(tril, colv - rowv, -jnp.inf))
                keep = first_half if half == 0 else jnp.logical_not(first_half)
                d = _dot((gmat * seg).astype(BF16), jnp.where(keep, x_pair, 0.0).astype(BF16))
                acc = d if acc is None else acc + d
            pieces.append(acc)
        ys.append(jnp.concatenate(pieces, axis=1) + y_off)
    y = jnp.concatenate(ys, axis=1) + xs * dsk_ref[...]

    zz = z_ref[...]
    u = y * (zz * _sigmoid(zz))
    for g in range(SSM_GROUPS):
        ug = _rms(u[:, g * gw:(g + 1) * gw], GATED_NORM_EPS)
        o_ref[:, g * gw:(g + 1) * gw] = (ug * ng_ref[:, g * gw:(g + 1) * gw]).astype(o_ref.dtype)


def _ssd(proj3, params):
    B, S, _ = proj3.shape
    L = SSM_CHUNK
    full = lambda a: pl.BlockSpec(a.shape, lambda b, c: (0,) * a.ndim)
    return pl.pallas_call(
        _ssd_kernel,
        grid=(B, S // L),
        in_specs=[
            pl.BlockSpec((None, L, SSM_CONV_CH), lambda b, c: (b, c, P_XBC // SSM_CONV_CH)),
            pl.BlockSpec((None, L, SSM_D_INNER), lambda b, c: (b, c, P_Z // SSM_D_INNER)),
            pl.BlockSpec((None, L, LANES), lambda b, c: (b, c, (P_TAIL + T_SM) // LANES)),
        ] + [full(a) for a in params],
        out_specs=pl.BlockSpec((None, L, SSM_D_INNER), lambda b, c: (b, c, 0)),
        out_shape=jax.ShapeDtypeStruct((B, S, SSM_D_INNER), BF16),
        scratch_shapes=[
            pltpu.VMEM((L + SUBLANES, SSM_CONV_CH), F32),
            pltpu.VMEM((SSM_STATE, SSM_D_INNER), F32),
        ],
        compiler_params=_cparams(("parallel", "arbitrary")),
        name="ssd",
    )(proj3, proj3, proj3, *params)


def _dsa_kernel(q_ref, qi_ref, wi_ref, k_ref, v_ref, ki_ref, o_ref,
                key_ref, k16_ref, pref_ref, cge_ref, cand_ref, cnt_ref, tie_ref,
                cand16_ref, cnt16_ref, s_ref, mx_ref, l_ref, acc_ref, *, t, topk):
    i = pl.program_id(1)
    nk = i + 1
    ng = t // LANES
    imin = jnp.int32(INT_MIN)
    ones = jnp.ones((LANES, LANES), BF16)
    row = lax.broadcasted_iota(I32, (t, t), 0)
    col = lax.broadcasted_iota(I32, (t, t), 1)

    def scores(j, diagonal):
        off = pl.multiple_of(j * t, t)
        kit = ki_ref[pl.ds(off, t), :]
        acc = jnp.zeros((t, t), F32)
        for h in range(IDX_HEADS):
            s = _dot_nt(qi_ref[:, h * LANES:(h + 1) * LANES], kit)
            acc = acc + wi_ref[:, SM_WI + h:SM_WI + h + 1] * jnp.maximum(s, 0.0)
        acc = jnp.where(acc == 0.0, 0.0, acc)
        bits = pltpu.bitcast(acc, I32)
        key = bits ^ ((bits >> 31) & jnp.int32(0x7FFFFFFF))
        if diagonal:
            key = jnp.where(col <= row, key, imin)
        for g in range(ng):
            kg = key[:, g * LANES:(g + 1) * LANES]
            key_ref[j * ng + g] = kg
            k16_ref[j * ng + g] = (kg >> 16).astype(jnp.int16)

    def scores_body(j, carry):
        scores(j, False)
        return carry

    lax.fori_loop(0, i, scores_body, 0)
    scores(i, True)

    def count(pred):
        cnt_ref[...] = jnp.zeros_like(cnt_ref)

        def body(j, carry):
            part = jnp.where(pred(key_ref[j * ng]), 1.0, 0.0)
            for g in range(1, ng):
                part = part + jnp.where(pred(key_ref[j * ng + g]), 1.0, 0.0)
            cnt_ref[...] += part
            return carry

        lax.fori_loop(0, nk, body, 0)
        return _dot(cnt_ref[...].astype(BF16), ones)

    kf = float(topk)
    half = 1 << 15
    one16, zero16 = jnp.ones((), BF16), jnp.zeros((), BF16)

    def count16(pred):
        cnt16_ref[...] = jnp.zeros_like(cnt16_ref)

        def body(j, carry):
            part = jnp.where(pred(k16_ref[j * ng]), one16, zero16)
            for g in range(1, ng):
                part = part + jnp.where(pred(k16_ref[j * ng + g]), one16, zero16)
            cnt16_ref[...] += part
            return carry

        lax.fori_loop(0, nk, body, 0)
        return _dot(cnt16_ref[...], ones)

    def search16(kneed):
        pref_ref[...] = jnp.zeros_like(pref_ref)
        cge_ref[...] = jnp.zeros_like(cge_ref)
        cnt_ref[...] = jnp.zeros_like(cnt_ref)

        def step(b, carry):
            cand_u = pref_ref[...] | lax.shift_left(jnp.int32(1), 15 - b)
            cand16_ref[...] = (cand_u - half).astype(jnp.int16)
            cnt = count16(lambda kt: kt >= cand16_ref[...])
            ok = cnt >= kneed
            pref_ref[...] = jnp.where(ok, cand_u, pref_ref[...])
            cge_ref[...] = jnp.where(ok, cnt, cge_ref[...])
            cnt_ref[...] = jnp.where(ok, cnt_ref[...], cnt)
            return carry

        lax.fori_loop(0, 16, step, 0)
        return pref_ref[...], cge_ref[...], cnt_ref[...]

    hi_u, cge_hi, cgt_hi = search16(kf)
    hi_s = hi_u - half
    need_lo = kf - cgt_hi

    cand_ref[...] = hi_s

    def fill_lo(jj, carry):
        key = key_ref[jj]
        lo_s = (key & jnp.int32(0xFFFF)) - half
        k16_ref[jj] = jnp.where((key >> 16) == cand_ref[...], lo_s, -half).astype(jnp.int16)
        return carry

    lax.fori_loop(0, nk * ng, fill_lo, 0)
    lo_u, cge_lo, _ = search16(need_lo)
    cand_ref[...] = hi_s * (1 << 16) + lo_u

    c_ge = jnp.where(lo_u != 0, (kf - need_lo) + cge_lo, cge_hi)
    tie_row = ((hi_u != 0) | (lo_u != 0)) & (c_ge > kf)
    tie_ref[...] = jnp.where(tie_row, 1.0, 0.0)
    n_tie = jnp.sum(tie_ref[...])

    @pl.when(n_tie > 0.0)
    def _():
        need = kf - count(lambda kt: kt > cand_ref[...])
        rr = lax.broadcasted_iota(I32, (LANES, LANES), 0)
        cc = lax.broadcasted_iota(I32, (LANES, LANES), 1)
        upper = jnp.where(rr < cc, 1.0, 0.0).astype(BF16)
        tied = tie_ref[...] > 0.0
        cnt_ref[...] = jnp.zeros_like(cnt_ref)

        def fix(jj, carry):
            kt = key_ref[jj]
            eq = (kt == cand_ref[...]) & tied
            eqb = jnp.where(eq, 1.0, 0.0).astype(BF16)
            rank = cnt_ref[...] + _dot(eqb, upper)
            drop = eq & (rank >= need)
            key_ref[jj] = jnp.where(drop, imin, kt)
            cnt_ref[...] += _dot(eqb, ones)
            return carry

        lax.fori_loop(0, nk * ng, fix, 0)

    cand_ref[...] = jnp.maximum(cand_ref[...], imin + 1)

    def head(h):
        lo, hi = h * HEAD_DIM, (h + 1) * HEAD_DIM

        def logits(j, is_tail):
            off = pl.multiple_of(j * t, t)
            s = _dot_nt(q_ref[:, lo:hi], k_ref[pl.ds(off, t), :])
            return jnp.concatenate(
                [jnp.where(key_ref[j * ng + g] >= cand_ref[...], s[:, g * LANES:(g + 1) * LANES], NEG_BIG)
                 for g in range(ng)], axis=1)

        def values(j):
            return v_ref[pl.ds(pl.multiple_of(j * t, t), t), :]

        return logits, values

    for h0 in range(0, DSA_HEADS, DSA_GROUP):
        outs = _softmax_attend([head(h0 + n) for n in range(DSA_GROUP)], nk, None, t,
                               s_ref, mx_ref, l_ref, acc_ref)
        for n, out in enumerate(outs):
            o_ref[:, (h0 + n) * HEAD_DIM:(h0 + n + 1) * HEAD_DIM] = out.astype(o_ref.dtype)


def _dsa(dq, qi, wi, dk, dv, ki, topk, *, t=512):
    B, S, W = dq.shape
    rep = lambda dt: pltpu.VMEM((t, LANES), dt)
    return pl.pallas_call(
        functools.partial(_dsa_kernel, t=t, topk=topk),
        grid=(B, S // t),
        in_specs=[
            pl.BlockSpec((None, t, W), lambda b, i: (b, i, 0)),
            pl.BlockSpec((None, t, IDX_W), lambda b, i: (b, i, 0)),
            pl.BlockSpec((None, t, LANES), lambda b, i: (b, i, 0)),
            pl.BlockSpec((None, S, HEAD_DIM), lambda b, i: (b, 0, 0)),
            pl.BlockSpec((None, S, HEAD_DIM), lambda b, i: (b, 0, 0)),
            pl.BlockSpec((None, S, LANES), lambda b, i: (b, 0, 0)),
        ],
        out_specs=pl.BlockSpec((None, t, W), lambda b, i: (b, i, 0)),
        out_shape=jax.ShapeDtypeStruct((B, S, W), BF16),
        scratch_shapes=[
            pltpu.VMEM((S // LANES, t, LANES), I32),
            pltpu.VMEM((S // LANES, t, LANES), jnp.int16),
            rep(I32), rep(F32), rep(I32), rep(F32), rep(F32),
            rep(jnp.int16), rep(BF16),
        ] + _attend_scratch(DSA_GROUP, S, t),
        compiler_params=_cparams(("parallel", "arbitrary")),
        name="dsa",
    )(dq, qi, wi, dk, dv, ki)


def _out_proj_kernel(x_ref, a_ref, s_ref, d_ref, w_ref, o_ref):
    acc = _dot(a_ref[...], w_ref[0:FOX_W, :])
    acc = acc + _dot(s_ref[...], w_ref[FOX_W:FOX_W + SSM_D_INNER, :])
    acc = acc + _dot(d_ref[...], w_ref[FOX_W + SSM_D_INNER:, :])
    o_ref[...] = x_ref[...] + acc


def _out_proj(x, fox_o, ssm_o, dsa_o, w, layer, *, tm=512):
    T, D = x.shape
    blk = lambda wd: pl.BlockSpec((tm, wd), lambda i: (i, 0))
    return pl.pallas_call(
        _out_proj_kernel,
        grid=(T // tm,),
        in_specs=[blk(D), blk(FOX_W), blk(SSM_D_INNER), blk(DSA_W),
                  pl.BlockSpec((None,) + w.shape[1:], lambda i: (layer, 0, 0))],
        out_specs=blk(D),
        out_shape=jax.ShapeDtypeStruct((T, D), F32),
        compiler_params=_cparams(("parallel",)),
        name="out_proj",
    )(x, fox_o, ssm_o, dsa_o, w)


def _relayout_w_in(w_in):
    sizes = (FOX_W, FOX_W, FOX_W, FOX_HEADS, SSM_D_INNER, SSM_CONV_CH, SSM_HEADS,
             DSA_Q_LORA, HEAD_DIM, HEAD_DIM, IDX_HEAD_DIM, IDX_HEADS)
    offs = [0]
    for s in sizes:
        offs.append(offs[-1] + s)
    w_in = w_in.astype(BF16)
    seg = lambda n: w_in[:, :, offs[n]:offs[n + 1]]
    fq, fk, fv, fg, z, xbc, dt, cq, dk, dv, dki, dwi = (seg(n) for n in range(len(sizes)))
    zeros = lambda n: jnp.zeros(w_in.shape[:2] + (n,), BF16)
    small = jnp.concatenate([dki, fg, dt, dwi, zeros(LANES - SM_WI - IDX_HEADS)], axis=-1)
    tail = jnp.concatenate([cq, dk, dv, small, zeros(TAIL_W - T_SM - LANES)], axis=-1)
    return jnp.concatenate([xbc, fq, fk, fv, z, tail], axis=-1)


def _rope_tables(S, head_dim):
    rot = head_dim // ROT_FRACTION
    half = rot // 2
    inv = ROPE_THETA ** (-jnp.arange(0, rot, 2, dtype=F32) / rot)
    ang = jnp.arange(S, dtype=F32)[:, None] * inv[None, :]
    cos, sin = jnp.cos(ang), jnp.sin(ang)
    ones = jnp.ones((S, head_dim - rot), F32)
    zer = jnp.zeros((S, head_dim - half), F32)
    c = jnp.concatenate([cos, cos, ones], axis=-1)
    s_lo = jnp.concatenate([-sin, zer], axis=-1)
    s_hi = jnp.concatenate([jnp.zeros((S, half), F32), sin, jnp.zeros((S, head_dim - rot), F32)], axis=-1)
    rep = LANES // head_dim
    return tuple(jnp.tile(t, (1, rep)) for t in (c, s_lo, s_hi))


def _lane_row(v, start):
    return jnp.zeros((1, LANES), F32).at[0, start:start + v.shape[0]].set(v.astype(F32))


def kernel(x, ffn1_norm, ffn1_w_gate, ffn1_w_up, ffn1_w_down, mix_norm, w_in, fox_fgate_b, fox_q_norm, fox_k_norm, ssm_conv_w, ssm_conv_b, ssm_dt_bias, ssm_a_log, ssm_d, ssm_norm, dsa_cq_norm, dsa_w_uq, dsa_w_uq_idx, dsa_q_norm, dsa_k_norm, dsa_kidx_norm, w_out, ffn2_norm, ffn2_w_gate, ffn2_w_up, ffn2_w_down):
    B, S, D = x.shape
    assert D == D_MODEL and S % 512 == 0 and (B * S) % 1024 == 0
    depth = w_in.shape[0]
    T = B * S
    topk = min(TOPK_MAX, S // 4)

    w_in_p = _relayout_w_in(w_in)
    bf = lambda a: a.astype(BF16)
    f1g, f1u, f1d = bf(ffn1_w_gate), bf(ffn1_w_up), bf(ffn1_w_down)
    f2g, f2u, f2d = bf(ffn2_w_gate), bf(ffn2_w_up), bf(ffn2_w_down)
    wuq, wo = bf(dsa_w_uq), bf(w_out)
    wuqi = bf(dsa_w_uq_idx).reshape(depth, DSA_Q_LORA, IDX_HEADS, IDX_HEAD_DIM)
    wuqi = jnp.pad(wuqi, ((0, 0), (0, 0), (0, 0), (0, LANES - IDX_HEAD_DIM))).reshape(depth, DSA_Q_LORA, IDX_W)
    tab_a = _rope_tables(S, HEAD_DIM)
    tab_i = _rope_tables(S, IDX_HEAD_DIM)

    xt = x.reshape(T, D)
    for l in range(depth):
        xt = _ffn(xt, ffn1_norm[l][None], f1g, f1u, f1d, l)

        proj = _in_proj(xt, mix_norm[l][None], w_in_p, l)
        prep_params = (
            fox_q_norm[l][None], fox_k_norm[l][None], _lane_row(fox_fgate_b[l], SM_FG),
            dsa_cq_norm[l][None],
            dsa_q_norm[l][None], dsa_k_norm[l][None], _lane_row(dsa_kidx_norm[l], SM_KI),
        )
        fq, fk, fv, sm, dq, qi, dk, dv, ki = _prep(proj, prep_params, wuq, wuqi, l, tab_a + tab_i, S)

        r3 = lambda a: a.reshape(B, S, a.shape[-1])
        cum = _cumsum(r3(sm))
        c_row = jnp.swapaxes(cum[:, :, SM_FG:SM_FG + FOX_HEADS], 1, 2)
        fox_o = _fox(r3(fq), r3(fk), r3(fv), c_row)

        dtb_l = _lane_row(ssm_dt_bias[l], SM_DT)
        alog_l = _lane_row(ssm_a_log[l], SM_DT)
        ssd_params = (
            ssm_conv_w[l], ssm_conv_b[l][None], dtb_l, alog_l,
            jnp.repeat(ssm_d[l], SSM_HEAD_DIM)[None], ssm_norm[l][None],
        )
        ssm_o = _ssd(r3(proj), ssd_params)

        dsa_o = _dsa(r3(dq), r3(qi), r3(sm), r3(dk), r3(dv), r3(ki), topk)

        xt = _out_proj(xt, fox_o.reshape(T, FOX_W), ssm_o.reshape(T, SSM_D_INNER),
                       dsa_o.reshape(T, DSA_W), wo, l)

        xt = _ffn(xt, ffn2_norm[l][None], f2g, f2u, f2d, l)
    return xt.reshape(B, S, D)
```
